```python
import math
import jax, jax.numpy as jnp
from jax import lax
import numpy as np

D_MODEL = 1024
BATCH = 2
SEQ = 16384
DEPTH = 2

SSM_EXPAND = 2
SSM_D_INNER = SSM_EXPAND * D_MODEL
SSM_HEAD_DIM = 64
SSM_HEADS = SSM_D_INNER // SSM_HEAD_DIM
SSM_GROUPS = 8
SSM_STATE = 128
SSM_CONV = 4
SSM_CHUNK = 256
SSM_XBC = SSM_D_INNER + 2 * SSM_GROUPS * SSM_STATE
SC_WIDTH = D_MODEL
SC_CONV = 3
D_IN_PROJ = SSM_D_INNER + SSM_XBC + SSM_HEADS + 4 * SC_WIDTH
SPLIT_Z = SSM_D_INNER
SPLIT_XBC = SPLIT_Z + SSM_XBC
SPLIT_DT = SPLIT_XBC + SSM_HEADS
SPLIT_SCB = SPLIT_DT + SC_WIDTH
SPLIT_SCC = SPLIT_SCB + SC_WIDTH
SPLIT_SCX = SPLIT_SCC + SC_WIDTH
N_BRANCH = 2
DN_ALPHA = (2 * DEPTH) ** 0.25
DN_BETA = (8 * DEPTH) ** -0.25
LN_EPS = 1e-5
RMS_EPS = 1e-5
DT_MIN = 1e-3
DT_MAX = 1e-1

kernel_name = "hybrid_ssd_shortconv_gated_merge"


def layer_norm(x, g=None, b=None):
    xf = x.astype(jnp.float32)
    mu = jnp.mean(xf, axis=-1, keepdims=True)
    xc = xf - mu
    var = jnp.mean(xc * xc, axis=-1, keepdims=True)
    y = xc * lax.rsqrt(var + LN_EPS)
    if g is not None:
        y = y * g.astype(jnp.float32) + b.astype(jnp.float32)
    return y.astype(x.dtype)


def causal_depthwise_conv(x, w, b=None):
    k = w.shape[0]
    ch = x.shape[-1]
    y = lax.conv_general_dilated(
        x, w[:, None, :].astype(x.dtype), window_strides=(1,), padding=[(k - 1, 0)],
        dimension_numbers=("NWC", "WIO", "NWC"), feature_group_count=ch)
    if b is not None:
        y = y + b
    return y


def ssd_chunked_scan(xdt, a, bm, cm):
    bsz, s, h, p = xdt.shape
    g, n = bm.shape[2], bm.shape[3]
    r = h // g
    L = SSM_CHUNK
    pad = (-s) % L
    if pad:
        pw = lambda t: jnp.pad(t, [(0, 0), (0, pad)] + [(0, 0)] * (t.ndim - 2))
        xdt, a, bm, cm = pw(xdt), pw(a), pw(bm), pw(cm)
    sp = s + pad
    nc = sp // L
    xc = xdt.reshape(bsz, nc, L, g, r, p).transpose(1, 0, 2, 3, 4, 5)
    ac = a.reshape(bsz, nc, L, g, r).transpose(1, 0, 2, 3, 4)
    bc = bm.reshape(bsz, nc, L, g, n).transpose(1, 0, 2, 3, 4)
    cc = cm.reshape(bsz, nc, L, g, n).transpose(1, 0, 2, 3, 4)
    causal = jnp.tril(jnp.ones((L, L), dtype=bool))[None, :, :, None, None]

    def step(state, inp):
        x_, a_, b_, c_ = inp
        a_cum = jnp.cumsum(a_, axis=1)
        seg = a_cum[:, :, None] - a_cum[:, None, :]
        decay = jnp.exp(jnp.where(causal, seg, -jnp.inf))
        cb = jnp.einsum("blgn,bsgn->blsg", c_, b_)
        y_diag = jnp.einsum("blsg,blsgr,bsgrp->blgrp", cb, decay, x_)
        y_off = jnp.einsum("blgn,bgrpn->blgrp", c_, state) * jnp.exp(a_cum)[..., None]
        a_end = a_cum[:, -1]
        to_end = jnp.exp(a_end[:, None] - a_cum)
        new_state = state * jnp.exp(a_end)[..., None, None] + jnp.einsum(
            "bsgn,bsgr,bsgrp->bgrpn", b_, to_end, x_)
        return new_state, y_diag + y_off

    h0 = jnp.zeros((bsz, g, r, p, n), jnp.float32)
    _, y = lax.scan(step, h0, (xc, ac, bc, cc))
    y = y.transpose(1, 0, 2, 3, 4, 5).reshape(bsz, sp, h, p)
    return y[:, :s]


def hybrid_layer(x, c, w_mod, b_mod, w_in, conv_ssm_w, conv_ssm_b, dt_bias, a_log,
                 d_skip, ssm_norm_w, conv_sc_w, w_branch_a, w_branch_b, w_gate,
                 b_gate, w_out, ln_g, ln_b):
    bsz, s, _ = x.shape
    mod = c @ w_mod + b_mod
    shift, scale, gate = jnp.split(mod, 3, axis=-1)
    u = layer_norm(x) * (1 + scale[:, None, :]) + shift[:, None, :]

    proj = u @ w_in
    z, xbc, dt_raw, sc_b, sc_c, sc_x, sc_g = jnp.split(
        proj, [SPLIT_Z, SPLIT_XBC, SPLIT_DT, SPLIT_SCB, SPLIT_SCC, SPLIT_SCX], axis=-1)

    xbc = jax.nn.silu(causal_depthwise_conv(xbc, conv_ssm_w, conv_ssm_b))
    xs, bm, cm = jnp.split(xbc, [SSM_D_INNER, SSM_D_INNER + SSM_GROUPS * SSM_STATE], axis=-1)
    dt = jax.nn.softplus(dt_raw.astype(jnp.float32) + dt_bias.astype(jnp.float32))
    a_head = -jnp.exp(a_log.astype(jnp.float32))
    xh = xs.reshape(bsz, s, SSM_HEADS, SSM_HEAD_DIM).astype(jnp.float32)
    y = ssd_chunked_scan(
        xh * dt[..., None], dt * a_head,
        bm.reshape(bsz, s, SSM_GROUPS, SSM_STATE).astype(jnp.float32),
        cm.reshape(bsz, s, SSM_GROUPS, SSM_STATE).astype(jnp.float32))
    y = y + d_skip.astype(jnp.float32)[:, None] * xh
    y = y.reshape(bsz, s, SSM_D_INNER) * jax.nn.silu(z.astype(jnp.float32))
    yg = y.reshape(bsz, s, SSM_GROUPS, SSM_D_INNER // SSM_GROUPS)
    yg = yg * lax.rsqrt(jnp.mean(yg * yg, axis=-1, keepdims=True) + RMS_EPS)
    y = yg.reshape(bsz, s, SSM_D_INNER) * ssm_norm_w.astype(jnp.float32)
    y_a = y.astype(x.dtype) @ w_branch_a

    v = causal_depthwise_conv(sc_c * sc_x, conv_sc_w)
    y_b = (jax.nn.silu(sc_g) * sc_b * v) @ w_branch_b

    gates = jax.nn.sigmoid(u @ w_gate + b_gate)
    g_a, g_b = jnp.split(gates, N_BRANCH, axis=-1)
    mixed = (g_a * y_a + g_b * y_b) @ w_out

    return layer_norm(DN_ALPHA * x + gate[:, None, :] * mixed, ln_g, ln_b)


def setup_inputs(seed: int = 0) -> dict:
    key = jax.random.key(seed)
    ks = jax.random.split(key, 20)
    f32 = jnp.float32
    nrm = lambda k, shape, sc: jax.random.normal(k, shape, f32) * sc
    u_dt = jax.random.uniform(ks[7], (DEPTH, SSM_HEADS), f32)
    dt0 = jnp.exp(u_dt * (math.log(DT_MAX) - math.log(DT_MIN)) + math.log(DT_MIN))
    dt_bias = dt0 + jnp.log(-jnp.expm1(-dt0))
    return {
        "x": nrm(ks[0], (BATCH, SEQ, D_MODEL), 1.0),
        "c": nrm(ks[1], (BATCH, D_MODEL), 1.0),
        "w_mod": nrm(ks[2], (DEPTH, D_MODEL, 3 * D_MODEL), 0.5 * D_MODEL ** -0.5),
        "b_mod": nrm(ks[3], (DEPTH, 3 * D_MODEL), 0.01),
        "w_in": nrm(ks[4], (DEPTH, D_MODEL, D_IN_PROJ), D_MODEL ** -0.5),
        "conv_ssm_w": nrm(ks[5], (DEPTH, SSM_CONV, SSM_XBC), SSM_CONV ** -0.5),
        "conv_ssm_b": nrm(ks[6], (DEPTH, SSM_XBC), 0.01),
        "dt_bias": dt_bias,
        "a_log": jnp.log(jax.random.uniform(ks[8], (DEPTH, SSM_HEADS), f32, 1.0, 16.0)),
        "d_skip": 1.0 + nrm(ks[9], (DEPTH, SSM_HEADS), 0.1),
        "ssm_norm_w": 1.0 + nrm(ks[10], (DEPTH, SSM_D_INNER), 0.1),
        "conv_sc_w": nrm(ks[11], (DEPTH, SC_CONV, SC_WIDTH), SC_CONV ** -0.5),
        "w_branch_a": nrm(ks[12], (DEPTH, SSM_D_INNER, D_MODEL), DN_BETA * SSM_D_INNER ** -0.5),
        "w_branch_b": nrm(ks[13], (DEPTH, SC_WIDTH, D_MODEL), DN_BETA * SC_WIDTH ** -0.5),
        "w_gate": nrm(ks[14], (DEPTH, D_MODEL, N_BRANCH * D_MODEL), D_MODEL ** -0.5),
        "b_gate": nrm(ks[15], (DEPTH, N_BRANCH * D_MODEL), 0.01),
        "w_out": nrm(ks[16], (DEPTH, D_MODEL, D_MODEL), DN_BETA * D_MODEL ** -0.5),
        "ln_g": 1.0 + nrm(ks[17], (DEPTH, D_MODEL), 0.1),
        "ln_b": nrm(ks[18], (DEPTH, D_MODEL), 0.01),
    }


def reference(x, c, w_mod, b_mod, w_in, conv_ssm_w, conv_ssm_b, dt_bias, a_log,
              d_skip, ssm_norm_w, conv_sc_w, w_branch_a, w_branch_b, w_gate,
              b_gate, w_out, ln_g, ln_b):
    for l in range(DEPTH):
        x = hybrid_layer(x, c, w_mod[l], b_mod[l], w_in[l], conv_ssm_w[l], conv_ssm_b[l],
                         dt_bias[l], a_log[l], d_skip[l], ssm_norm_w[l], conv_sc_w[l],
                         w_branch_a[l], w_branch_b[l], w_gate[l], b_gate[l], w_out[l],
                         ln_g[l], ln_b[l])
    return x
```

```python
import functools

import jax
import jax.numpy as jnp
from jax import lax
from jax.experimental import pallas as pl
from jax.experimental.pallas import tpu as pltpu

D_MODEL = 1024
DEPTH = 2
D_INNER = 2048
HEAD_DIM = 64
HEADS = 32
GROUPS = 8
STATE = 128
SSM_CONV = 4
CHUNK = 256
XBC = D_INNER + 2 * GROUPS * STATE
SC_WIDTH = 1024
SC_CONV = 3
HEADS_PER_GROUP = HEADS // GROUPS
GROUP_WIDTH = HEADS_PER_GROUP * HEAD_DIM
SPLIT_Z = D_INNER
SPLIT_XBC = SPLIT_Z + XBC
SPLIT_DT = SPLIT_XBC + HEADS
DN_ALPHA = (2 * DEPTH) ** 0.25
LN_EPS = 1e-5
RMS_EPS = 1e-5

LANES = 128
SUBLANES = 8
HALO = SUBLANES
COL_BLOCK = 256
HEAD_REPLICAS = 3
MERGE_TILE = 512
VMEM_LIMIT_BYTES = 56 * 1024 * 1024

F32 = jnp.float32
BF16 = jnp.bfloat16


def _dot(a, b):
    return jnp.dot(a, b, preferred_element_type=F32)


def _layer_norm(x):
    mu = jnp.mean(x, axis=-1, keepdims=True)
    xc = x - mu
    var = jnp.mean(xc * xc, axis=-1, keepdims=True)
    return xc * lax.rsqrt(var + LN_EPS)


def _silu(x):
    return x * jax.nn.sigmoid(x)


def _softplus(x):
    return jnp.maximum(x, 0.0) + jnp.log1p(jnp.exp(-jnp.abs(x)))


def _split3(x):
    hi = x.astype(BF16).astype(F32)
    r = x - hi
    mid = r.astype(BF16).astype(F32)
    return hi, mid, r - mid


def _lane_parts(x):
    hi, mid, lo = _split3(x)
    lane = lax.broadcasted_iota(jnp.int32, x.shape, 1)
    return jnp.where(lane < HEADS, hi, jnp.where(lane < 2 * HEADS, mid, lo)).astype(BF16)


def _mod_kernel(c_ref, w_ref, b_ref, o_ref):
    o_ref[...] = jnp.dot(c_ref[...], w_ref[...], precision=lax.Precision.HIGHEST,
                         preferred_element_type=F32) + b_ref[...]


def _modulation(c, w_mod, b_mod):
    bsz = c.shape[0]
    rows = -(-bsz // SUBLANES) * SUBLANES
    c_pad = jnp.zeros((rows, D_MODEL), F32).at[:bsz].set(c)
    out = pl.pallas_call(
        _mod_kernel,
        grid=(DEPTH,),
        in_specs=[
            pl.BlockSpec((rows, D_MODEL), lambda l: (0, 0)),
            pl.BlockSpec((None, D_MODEL, 3 * D_MODEL), lambda l: (l, 0, 0)),
            pl.BlockSpec((None, 1, 3 * D_MODEL), lambda l: (l, 0, 0)),
        ],
        out_specs=pl.BlockSpec((None, rows, 3 * D_MODEL), lambda l: (l, 0, 0)),
        out_shape=jax.ShapeDtypeStruct((DEPTH, rows, 3 * D_MODEL), F32),
        compiler_params=pltpu.CompilerParams(
            dimension_semantics=("arbitrary",), vmem_limit_bytes=VMEM_LIMIT_BYTES),
        name="adaln_modulation",
    )(c_pad, w_mod, b_mod.reshape(DEPTH, 1, 3 * D_MODEL))
    return out[:, :bsz]


def _mixer_kernel(x_ref, mod_ref, wz_ref, wxbc_ref, wdt_ref, wsc_ref, convw_ref, convb_ref,
                  dtb_ref, alog_ref, dskip_ref, normw_ref, convsc_ref, tri_ref, negmask_ref,
                  esel_ref, eexp_ref,
                  ynorm_ref, ybin_ref,
                  u_ref, tail_ref, sctail_ref, tmp_ref, xs_ref, xdtm_ref, xw_ref, b_ref, bt_ref,
                  c_ref, acum3_ref, acumt_ref, e3_ref, state_ref):
    L = CHUNK

    @pl.when(pl.program_id(1) == 0)
    def _():
        tail_ref[...] = jnp.zeros_like(tail_ref)
        sctail_ref[...] = jnp.zeros_like(sctail_ref)
        state_ref[...] = jnp.zeros_like(state_ref)

    shift = mod_ref[:, 0:D_MODEL]
    scale = mod_ref[:, D_MODEL:2 * D_MODEL]
    u_ref[...] = (_layer_norm(x_ref[...]) * (1.0 + scale) + shift).astype(BF16)

    dt = _softplus(_dot(u_ref[...], wdt_ref[...]) + dtb_ref[...])
    a = dt * (-jnp.exp(alog_ref[...]))
    a_hi, a_mid, a_lo = _split3(a)
    tri = tri_ref[...]
    a_cum = (_dot(tri, a_hi.astype(BF16)) + _dot(tri, a_mid.astype(BF16))
             + _dot(tri, a_lo.astype(BF16)))
    acum3_ref[...] = _lane_parts(a_cum)
    acumt_ref[...] = a_cum.T
    e3_ref[...] = _lane_parts(jnp.exp(a_cum))
    to_end = jnp.exp(a_cum[L - 1:L, :] - a_cum)
    dt3 = _lane_parts(dt)
    te3 = _lane_parts(to_end)

    lane = lax.broadcasted_iota(jnp.int32, (1, COL_BLOCK), 1)
    for blk in range(XBC // COL_BLOCK):
        cols = slice(blk * COL_BLOCK, (blk + 1) * COL_BLOCK)
        raw = _dot(u_ref[...], wxbc_ref[:, cols])
        buf = tmp_ref.at[blk % 2]
        buf[0:HALO, :] = tail_ref[:, cols]
        buf[HALO:HALO + L, :] = raw
        tail_ref[:, cols] = raw[L - HALO:L, :]
        acc = convb_ref[:, cols]
        for k in range(SSM_CONV):
            off = HALO - (SSM_CONV - 1) + k
            acc = acc + convw_ref[k:k + 1, cols] * buf[pl.ds(off, L), :]
        act = _silu(acc)
        if blk < GROUPS:
            g = blk
            xs_ref[g] = act
            xdt = act * _dot(dt3, eexp_ref[g])
            for hh in range(HEADS_PER_GROUP):
                in_head = (lane >= hh * HEAD_DIM) & (lane < (hh + 1) * HEAD_DIM)
                xdtm_ref[g, hh * L:(hh + 1) * L, :] = jnp.where(in_head, xdt, 0.0).astype(BF16)
            xw_ref[g] = (xdt * _dot(te3, eexp_ref[g])).astype(BF16)
        else:
            for j in range(COL_BLOCK // STATE):
                part = act[:, j * STATE:(j + 1) * STATE]
                g = (blk * COL_BLOCK - D_INNER) // STATE + j
                if g < GROUPS:
                    b_ref[g] = part.astype(BF16)
                    bt_ref[g] = part.T.astype(BF16)
                else:
                    c_ref[g - GROUPS] = part.astype(BF16)

    for blk in range(SC_WIDTH // COL_BLOCK):
        cols = slice(blk * COL_BLOCK, (blk + 1) * COL_BLOCK)
        u = u_ref[...]
        sc_b = _dot(u, wsc_ref[:, blk * COL_BLOCK:(blk + 1) * COL_BLOCK])
        sc_c = _dot(u, wsc_ref[:, SC_WIDTH + blk * COL_BLOCK:SC_WIDTH + (blk + 1) * COL_BLOCK])
        sc_x = _dot(u, wsc_ref[:, 2 * SC_WIDTH + blk * COL_BLOCK:2 * SC_WIDTH + (blk + 1) * COL_BLOCK])
        sc_g = _dot(u, wsc_ref[:, 3 * SC_WIDTH + blk * COL_BLOCK:3 * SC_WIDTH + (blk + 1) * COL_BLOCK])
        prod = sc_c * sc_x
        buf = tmp_ref.at[blk % 2]
        buf[0:HALO, :] = sctail_ref[:, cols]
        buf[HALO:HALO + L, :] = prod
        sctail_ref[:, cols] = prod[L - HALO:L, :]
        v = jnp.zeros((L, COL_BLOCK), F32)
        for k in range(SC_CONV):
            off = HALO - (SC_CONV - 1) + k
            v = v + convsc_ref[k:k + 1, cols] * buf[pl.ds(off, L), :]
        ybin_ref[:, cols] = (_silu(sc_g) * sc_b * v).astype(BF16)

    def group_body(g, carry):
        cg = c_ref[g]
        cb = lax.dot_general(cg, b_ref[g], (((1,), (1,)), ((), ())),
                             preferred_element_type=F32)
        acol4 = _dot(acum3_ref[...], esel_ref[g])
        ms = []
        for hh in range(HEADS_PER_GROUP):
            arow = acumt_ref[pl.ds(g * HEADS_PER_GROUP + hh, 1), :]
            acol = acol4[:, hh * LANES:(hh + 1) * LANES]
            seg = jnp.concatenate([acol, acol], axis=1) - arow + negmask_ref[...]
            ms.append((cb * jnp.exp(seg)).astype(BF16))
        y = _dot(jnp.concatenate(ms, axis=1), xdtm_ref[g])
        state = state_ref[g]
        escale = _dot(e3_ref[...], eexp_ref[g])
        y = y + _dot(cg, state.astype(BF16)) * escale
        state_ref[g] = state * escale[L - 1:L, :] + _dot(bt_ref[g], xw_ref[g])
        y = y + dskip_ref[g] * xs_ref[g]
        y = y * _silu(_dot(u_ref[...], wz_ref[g]))
        y = y * lax.rsqrt(jnp.mean(y * y, axis=-1, keepdims=True) + RMS_EPS)
        ynorm_ref[g] = (y * normw_ref[g]).astype(BF16)
        return carry

    lax.fori_loop(0, GROUPS, group_body, 0)


def _const(shape):
    return pl.BlockSpec(shape, lambda b, i: (0,) * len(shape), pipeline_mode=pl.Buffered(1))


def _mixer(x, mod, wz, wxbc, wdt, wsc, convw, convb, dtb, alog, dskip, normw, convsc,
           tri, negmask, esel, eexp):
    bsz, seq, _ = x.shape
    L = CHUNK
    assert seq % L == 0
    consts = (wz, wxbc, wdt, wsc, convw, convb, dtb, alog, dskip, normw, convsc, tri, negmask,
              esel, eexp)
    return pl.pallas_call(
        _mixer_kernel,
        grid=(bsz, seq // L),
        in_specs=[
            pl.BlockSpec((None, L, D_MODEL), lambda b, i: (b, i, 0)),
            pl.BlockSpec((None, 1, 3 * D_MODEL), lambda b, i: (b, 0, 0)),
        ] + [_const(a.shape) for a in consts],
        out_specs=[
            pl.BlockSpec((None, GROUPS, L, GROUP_WIDTH), lambda b, i: (b, 0, i, 0)),
            pl.BlockSpec((None, L, SC_WIDTH), lambda b, i: (b, i, 0)),
        ],
        out_shape=[
            jax.ShapeDtypeStruct((bsz, GROUPS, seq, GROUP_WIDTH), BF16),
            jax.ShapeDtypeStruct((bsz, seq, SC_WIDTH), BF16),
        ],
        scratch_shapes=[
            pltpu.VMEM((L, D_MODEL), BF16),
            pltpu.VMEM((HALO, XBC), F32),
            pltpu.VMEM((HALO, SC_WIDTH), F32),
            pltpu.VMEM((2, HALO + L, COL_BLOCK), F32),
            pltpu.VMEM((GROUPS, L, GROUP_WIDTH), F32),
            pltpu.VMEM((GROUPS, HEADS_PER_GROUP * L, GROUP_WIDTH), BF16),
            pltpu.VMEM((GROUPS, L, GROUP_WIDTH), BF16),
            pltpu.VMEM((GROUPS, L, STATE), BF16),
            pltpu.VMEM((GROUPS, STATE, L), BF16),
            pltpu.VMEM((GROUPS, L, STATE), BF16),
            pltpu.VMEM((L, LANES), BF16),
            pltpu.VMEM((LANES, L), F32),
            pltpu.VMEM((L, LANES), BF16),
            pltpu.VMEM((GROUPS, STATE, GROUP_WIDTH), F32),
        ],
        compiler_params=pltpu.CompilerParams(
            dimension_semantics=("arbitrary", "arbitrary"), vmem_limit_bytes=VMEM_LIMIT_BYTES),
        name="ssd_shortconv_mixer",
    )(x, mod, *consts)


def _merge_kernel(x_ref, mod_ref, ynorm_ref, ybin_ref, wgate_ref, bgate_ref, wa_ref, wb_ref,
                  wout_ref, lng_ref, lnb_ref, o_ref):
    x = x_ref[...]
    shift = mod_ref[:, 0:D_MODEL]
    scale = mod_ref[:, D_MODEL:2 * D_MODEL]
    gate = mod_ref[:, 2 * D_MODEL:3 * D_MODEL]
    u = (_layer_norm(x) * (1.0 + scale) + shift).astype(BF16)
    gates = jax.nn.sigmoid(_dot(u, wgate_ref[...]) + bgate_ref[...])
    y_a = _dot(ynorm_ref[0], wa_ref[0])
    for g in range(1, GROUPS):
        y_a = y_a + _dot(ynorm_ref[g], wa_ref[g])
    y_b = _dot(ybin_ref[...], wb_ref[...])
    merged = gates[:, 0:D_MODEL] * y_a + gates[:, D_MODEL:2 * D_MODEL] * y_b
    mixed = _dot(merged.astype(BF16), wout_ref[...])
    o_ref[...] = _layer_norm(DN_ALPHA * x + gate * mixed) * lng_ref[...] + lnb_ref[...]


def _merge(x, mod, ynorm, ybin, wgate, bgate, wa, wb, wout, lng, lnb):
    bsz, seq, _ = x.shape
    tm = min(MERGE_TILE, seq)
    assert seq % tm == 0
    consts = (wgate, bgate, wa, wb, wout, lng, lnb)
    return pl.pallas_call(
        _merge_kernel,
        grid=(bsz, seq // tm),
        in_specs=[
            pl.BlockSpec((None, tm, D_MODEL), lambda b, i: (b, i, 0)),
            pl.BlockSpec((None, 1, 3 * D_MODEL), lambda b, i: (b, 0, 0)),
            pl.BlockSpec((None, GROUPS, tm, GROUP_WIDTH), lambda b, i: (b, 0, i, 0)),
            pl.BlockSpec((None, tm, SC_WIDTH), lambda b, i: (b, i, 0)),
        ] + [_const(a.shape) for a in consts],
        out_specs=pl.BlockSpec((None, tm, D_MODEL), lambda b, i: (b, i, 0)),
        out_shape=jax.ShapeDtypeStruct((bsz, seq, D_MODEL), F32),
        compiler_params=pltpu.CompilerParams(
            dimension_semantics=("arbitrary", "arbitrary"), vmem_limit_bytes=VMEM_LIMIT_BYTES),
        name="gated_merge",
    )(x, mod, ynorm, ybin, *consts)


def _head_lanes(v):
    pad = jnp.zeros((LANES - HEAD_REPLICAS * HEADS,), F32)
    return jnp.concatenate([v] * HEAD_REPLICAS + [pad]).reshape(1, LANES)


def _selection_constants():
    L = CHUNK
    row = lax.broadcasted_iota(jnp.int32, (L, L), 0)
    col = lax.broadcasted_iota(jnp.int32, (L, L), 1)
    tri = (row >= col).astype(BF16)
    negmask = jnp.where(row >= col, 0.0, -jnp.inf).astype(F32)
    j = jnp.arange(LANES)
    head_of_lane = jnp.where(j < HEAD_REPLICAS * HEADS, j % HEADS, -1)
    ch_head = jnp.arange(D_INNER) // HEAD_DIM
    eexp = (head_of_lane[:, None] == ch_head[None, :]).astype(BF16)
    eexp = eexp.reshape(LANES, GROUPS, GROUP_WIDTH).transpose(1, 0, 2)
    sel_head = jnp.arange(HEADS * LANES) // LANES
    esel = (head_of_lane[:, None] == sel_head[None, :]).astype(BF16)
    esel = esel.reshape(LANES, GROUPS, HEADS_PER_GROUP * LANES).transpose(1, 0, 2)
    return tri, negmask, esel, eexp


def kernel(x, c, w_mod, b_mod, w_in, conv_ssm_w, conv_ssm_b, dt_bias, a_log, d_skip, ssm_norm_w,
           conv_sc_w, w_branch_a, w_branch_b, w_gate, b_gate, w_out, ln_g, ln_b):
    bsz = x.shape[0]
    mod_all = _modulation(c, w_mod, b_mod)
    tri, negmask, esel, eexp = _selection_constants()
    for l in range(DEPTH):
        mod = mod_all[l].reshape(bsz, 1, 3 * D_MODEL)
        w = w_in[l]
        wz = w[:, :SPLIT_Z].astype(BF16).reshape(D_MODEL, GROUPS, GROUP_WIDTH).transpose(1, 0, 2)
        wxbc = w[:, SPLIT_Z:SPLIT_XBC].astype(BF16)
        wdt_h = w[:, SPLIT_XBC:SPLIT_DT].astype(BF16)
        wdt = jnp.concatenate(
            [wdt_h] * HEAD_REPLICAS + [jnp.zeros((D_MODEL, LANES - HEAD_REPLICAS * HEADS), BF16)], axis=1)
        wsc = w[:, SPLIT_DT:].astype(BF16)
        ynorm, ybin = _mixer(
            x, mod, wz, wxbc, wdt, wsc,
            conv_ssm_w[l], conv_ssm_b[l].reshape(1, XBC),
            _head_lanes(dt_bias[l]), _head_lanes(a_log[l]),
            jnp.repeat(d_skip[l], HEAD_DIM).reshape(GROUPS, 1, GROUP_WIDTH),
            ssm_norm_w[l].reshape(GROUPS, 1, GROUP_WIDTH),
            conv_sc_w[l], tri, negmask, esel, eexp)
        x = _merge(
            x, mod, ynorm, ybin,
            w_gate[l].astype(BF16), b_gate[l].reshape(1, 2 * D_MODEL),
            w_branch_a[l].astype(BF16).reshape(GROUPS, GROUP_WIDTH, D_MODEL),
            w_branch_b[l].astype(BF16), w_out[l].astype(BF16),
            ln_g[l].reshape(1, D_MODEL), ln_b[l].reshape(1, D_MODEL))
    return x
```

```python
import functools

import jax
import jax.numpy as jnp
from jax import lax
from jax.experimental import pallas as pl
from jax.experimental.pallas import tpu as pltpu

D_MODEL = 1024
DEPTH = 2
D_INNER = 2048
HEAD_DIM = 64
HEADS = 32
GROUPS = 8
STATE = 128
SSM_CONV = 4
CHUNK = 256
XBC = D_INNER + 2 * GROUPS * STATE
SC_WIDTH = 1024
SC_CONV = 3
HEADS_PER_GROUP = HEADS // GROUPS
GROUP_WIDTH = HEADS_PER_GROUP * HEAD_DIM
SPLIT_Z = D_INNER
SPLIT_XBC = SPLIT_Z + XBC
SPLIT_DT = SPLIT_XBC + HEADS
DN_ALPHA = (2 * DEPTH) ** 0.25
LN_EPS = 1e-5
RMS_EPS = 1e-5

LANES = 128
SUBLANES = 8
ROW_TILES = CHUNK // SUBLANES
SSM_HALO = SUBLANES * (SSM_CONV - 1)
SC_HALO = SUBLANES * (SC_CONV - 1)
X_COL_BLOCKS = D_MODEL // LANES
COL_BLOCK = 256
HEAD_REPLICAS = 3
MERGE_TILE = 512
VMEM_LIMIT_BYTES = 56 * 1024 * 1024

F32 = jnp.float32
BF16 = jnp.bfloat16


def _dot(a, b):
    return jnp.dot(a, b, preferred_element_type=F32)


def _layer_norm(x):
    mu = jnp.mean(x, axis=-1, keepdims=True)
    xc = x - mu
    var = jnp.mean(xc * xc, axis=-1, keepdims=True)
    return xc * lax.rsqrt(var + LN_EPS)


def _silu(x):
    return x * jax.nn.sigmoid(x)


def _softplus(x):
    return jnp.maximum(x, 0.0) + jnp.log1p(jnp.exp(-jnp.abs(x)))


def _split3(x):
    hi = x.astype(BF16).astype(F32)
    r = x - hi
    mid = r.astype(BF16).astype(F32)
    return hi, mid, r - mid


def _lane_parts(x):
    hi, mid, lo = _split3(x)
    lane = lax.broadcasted_iota(jnp.int32, x.shape, 1)
    return jnp.where(lane < HEADS, hi, jnp.where(lane < 2 * HEADS, mid, lo)).astype(BF16)


def _load_time_interleaved(col_refs):
    return jnp.concatenate(
        [jnp.concatenate([ref[pl.ds(r, SUBLANES, stride=ROW_TILES), :] for r in range(ROW_TILES)],
                         axis=0) for ref in col_refs], axis=1)


def _store_time_ordered(stage_ref, value, out_ref, cols):
    for j in range(value.shape[1] // LANES):
        stage_ref[j] = value[:, j * LANES:(j + 1) * LANES]
    out_ref[:, cols] = jnp.concatenate(
        [jnp.concatenate([stage_ref.at[j][pl.ds(s, ROW_TILES, stride=SUBLANES), :]
                          for s in range(SUBLANES)], axis=0)
         for j in range(value.shape[1] // LANES)], axis=1).astype(out_ref.dtype)


def _conv_halo(prev_tail, cur_tail):
    rows = cur_tail.shape[0]
    sub = lax.broadcasted_iota(jnp.int32, cur_tail.shape, 0) % SUBLANES
    return jnp.where(sub == 0,
                     pltpu.roll(prev_tail, rows - (SUBLANES - 1), axis=0),
                     pltpu.roll(cur_tail, 1, axis=0))


def _causal_conv(buf, raw, tail_ref, cols, w_ref, taps):
    halo = SUBLANES * (taps - 1)
    cur_tail = raw[CHUNK - halo:CHUNK, :]
    buf[0:halo, :] = _conv_halo(tail_ref[:, cols], cur_tail)
    buf[halo:halo + CHUNK, :] = raw
    tail_ref[:, cols] = cur_tail
    acc = w_ref[taps - 1:taps, cols] * raw
    for k in range(taps - 1):
        acc = acc + w_ref[k:k + 1, cols] * buf[pl.ds(SUBLANES * k, CHUNK), :]
    return acc


def _mod_kernel(c_ref, w_ref, b_ref, o_ref):
    o_ref[...] = jnp.dot(c_ref[...], w_ref[...], precision=lax.Precision.HIGHEST,
                         preferred_element_type=F32) + b_ref[...]


def _modulation(c, w_mod, b_mod):
    bsz = c.shape[0]
    rows = -(-bsz // SUBLANES) * SUBLANES
    c_pad = jnp.zeros((rows, D_MODEL), F32).at[:bsz].set(c)
    out = pl.pallas_call(
        _mod_kernel,
        grid=(DEPTH,),
        in_specs=[
            pl.BlockSpec((rows, D_MODEL), lambda l: (0, 0)),
            pl.BlockSpec((None, D_MODEL, 3 * D_MODEL), lambda l: (l, 0, 0)),
            pl.BlockSpec((None, 1, 3 * D_MODEL), lambda l: (l, 0, 0)),
        ],
        out_specs=pl.BlockSpec((None, rows, 3 * D_MODEL), lambda l: (l, 0, 0)),
        out_shape=jax.ShapeDtypeStruct((DEPTH, rows, 3 * D_MODEL), F32),
        compiler_params=pltpu.CompilerParams(
            dimension_semantics=("arbitrary",), vmem_limit_bytes=VMEM_LIMIT_BYTES),
        name="adaln_modulation",
    )(c_pad, w_mod, b_mod.reshape(DEPTH, 1, 3 * D_MODEL))
    return out[:, :bsz]


def _mixer_kernel(*refs):
    x_cols = refs[:X_COL_BLOCKS]
    (mod_ref, wz_ref, wxbc_ref, wdt_ref, wsc_ref, convw_ref, convb_ref,
     dtb_ref, alog_ref, dskip_ref, normw_ref, convsc_ref, tri_ref, negmask_ref,
     esel_ref, eexp_ref,
     ynorm_ref, ybin_ref,
     u_ref, tail_ref, sctail_ref, tmp_ref, yout_ref, xs_ref, xdtm_ref, xw_ref, b_ref,
     bt_ref, c_ref, acum3_ref, acumt_ref, e3_ref, state_ref) = refs[X_COL_BLOCKS:]
    L = CHUNK

    @pl.when(pl.program_id(1) == 0)
    def _():
        tail_ref[...] = jnp.zeros_like(tail_ref)
        sctail_ref[...] = jnp.zeros_like(sctail_ref)
        state_ref[...] = jnp.zeros_like(state_ref)

    shift = mod_ref[:, 0:D_MODEL]
    scale = mod_ref[:, D_MODEL:2 * D_MODEL]
    u_ref[...] = (_layer_norm(_load_time_interleaved(x_cols)) * (1.0 + scale) + shift).astype(BF16)

    dt = _softplus(_dot(u_ref[...], wdt_ref[...]) + dtb_ref[...])
    a = dt * (-jnp.exp(alog_ref[...]))
    a_hi, a_mid, a_lo = _split3(a)
    tri = tri_ref[...]
    a_cum = (_dot(tri, a_hi.astype(BF16)) + _dot(tri, a_mid.astype(BF16))
             + _dot(tri, a_lo.astype(BF16)))
    acum3_ref[...] = _lane_parts(a_cum)
    acumt_ref[...] = a_cum.T
    e3_ref[...] = _lane_parts(jnp.exp(a_cum))
    to_end = jnp.exp(a_cum[L - 1:L, :] - a_cum)
    dt3 = _lane_parts(dt)
    te3 = _lane_parts(to_end)

    lane = lax.broadcasted_iota(jnp.int32, (1, COL_BLOCK), 1)
    for blk in range(XBC // COL_BLOCK):
        cols = slice(blk * COL_BLOCK, (blk + 1) * COL_BLOCK)
        raw = _dot(u_ref[...], wxbc_ref[:, cols])
        act = _silu(_causal_conv(tmp_ref.at[blk % 2], raw, tail_ref, cols, convw_ref, SSM_CONV)
                    + convb_ref[:, cols])
        if blk < GROUPS:
            g = blk
            xs_ref[g] = act
            xdt = act * _dot(dt3, eexp_ref[g])
            for hh in range(HEADS_PER_GROUP):
                in_head = (lane >= hh * HEAD_DIM) & (lane < (hh + 1) * HEAD_DIM)
                xdtm_ref[g, hh * L:(hh + 1) * L, :] = jnp.where(in_head, xdt, 0.0).astype(BF16)
            xw_ref[g] = (xdt * _dot(te3, eexp_ref[g])).astype(BF16)
        else:
            for j in range(COL_BLOCK // STATE):
                part = act[:, j * STATE:(j + 1) * STATE]
                g = (blk * COL_BLOCK - D_INNER) // STATE + j
                if g < GROUPS:
                    b_ref[g] = part.astype(BF16)
                    bt_ref[g] = part.T.astype(BF16)
                else:
                    c_ref[g - GROUPS] = part.astype(BF16)

    for blk in range(SC_WIDTH // COL_BLOCK):
        cols = slice(blk * COL_BLOCK, (blk + 1) * COL_BLOCK)
        u = u_ref[...]
        sc_b = _dot(u, wsc_ref[:, blk * COL_BLOCK:(blk + 1) * COL_BLOCK])
        sc_c = _dot(u, wsc_ref[:, SC_WIDTH + blk * COL_BLOCK:SC_WIDTH + (blk + 1) * COL_BLOCK])
        sc_x = _dot(u, wsc_ref[:, 2 * SC_WIDTH + blk * COL_BLOCK:2 * SC_WIDTH + (blk + 1) * COL_BLOCK])
        sc_g = _dot(u, wsc_ref[:, 3 * SC_WIDTH + blk * COL_BLOCK:3 * SC_WIDTH + (blk + 1) * COL_BLOCK])
        v = _causal_conv(tmp_ref.at[blk % 2], sc_c * sc_x, sctail_ref, cols, convsc_ref, SC_CONV)
        _store_time_ordered(yout_ref.at[blk % 2], _silu(sc_g) * sc_b * v, ybin_ref, cols)

    def group_body(g, carry):
        cg = c_ref[g]
        cb = lax.dot_general(cg, b_ref[g], (((1,), (1,)), ((), ())),
                             preferred_element_type=F32)
        acol4 = _dot(acum3_ref[...], esel_ref[g])
        ms = []
        for hh in range(HEADS_PER_GROUP):
            arow = acumt_ref[pl.ds(g * HEADS_PER_GROUP + hh, 1), :]
            acol = acol4[:, hh * LANES:(hh + 1) * LANES]
            seg = jnp.concatenate([acol, acol], axis=1) - arow + negmask_ref[...]
            ms.append((cb * jnp.exp(seg)).astype(BF16))
        y = _dot(jnp.concatenate(ms, axis=1), xdtm_ref[g])
        state = state_ref[g]
        escale = _dot(e3_ref[...], eexp_ref[g])
        y = y + _dot(cg, state.astype(BF16)) * escale
        state_ref[g] = state * escale[L - 1:L, :] + _dot(bt_ref[g], xw_ref[g])
        y = y + dskip_ref[g] * xs_ref[g]
        y = y * _silu(_dot(u_ref[...], wz_ref[g]))
        y = y * lax.rsqrt(jnp.mean(y * y, axis=-1, keepdims=True) + RMS_EPS)
        _store_time_ordered(yout_ref.at[0], y * normw_ref[g], ynorm_ref.at[g], slice(None))
        return carry

    lax.fori_loop(0, GROUPS, group_body, 0)


def _const(shape):
    return pl.BlockSpec(shape, lambda b, i: (0,) * len(shape), pipeline_mode=pl.Buffered(1))


def _mixer(x, mod, wz, wxbc, wdt, wsc, convw, convb, dtb, alog, dskip, normw, convsc,
           tri, negmask, esel, eexp):
    bsz, seq, _ = x.shape
    L = CHUNK
    assert seq % L == 0
    consts = (wz, wxbc, wdt, wsc, convw, convb, dtb, alog, dskip, normw, convsc, tri, negmask,
              esel, eexp)
    return pl.pallas_call(
        _mixer_kernel,
        grid=(bsz, seq // L),
        in_specs=[
            pl.BlockSpec((None, L, LANES), functools.partial(lambda j, b, i: (b, i, j), j))
            for j in range(X_COL_BLOCKS)
        ] + [
            pl.BlockSpec((None, 1, 3 * D_MODEL), lambda b, i: (b, 0, 0)),
        ] + [_const(a.shape) for a in consts],
        out_specs=[
            pl.BlockSpec((None, GROUPS, L, GROUP_WIDTH), lambda b, i: (b, 0, i, 0)),
            pl.BlockSpec((None, L, SC_WIDTH), lambda b, i: (b, i, 0)),
        ],
        out_shape=[
            jax.ShapeDtypeStruct((bsz, GROUPS, seq, GROUP_WIDTH), BF16),
            jax.ShapeDtypeStruct((bsz, seq, SC_WIDTH), BF16),
        ],
        scratch_shapes=[
            pltpu.VMEM((L, D_MODEL), BF16),
            pltpu.VMEM((SSM_HALO, XBC), F32),
            pltpu.VMEM((SC_HALO, SC_WIDTH), F32),
            pltpu.VMEM((2, SSM_HALO + L, COL_BLOCK), F32),
            pltpu.VMEM((2, COL_BLOCK // LANES, L, LANES), F32),
            pltpu.VMEM((GROUPS, L, GROUP_WIDTH), F32),
            pltpu.VMEM((GROUPS, HEADS_PER_GROUP * L, GROUP_WIDTH), BF16),
            pltpu.VMEM((GROUPS, L, GROUP_WIDTH), BF16),
            pltpu.VMEM((GROUPS, L, STATE), BF16),
            pltpu.VMEM((GROUPS, STATE, L), BF16),
            pltpu.VMEM((GROUPS, L, STATE), BF16),
            pltpu.VMEM((L, LANES), BF16),
            pltpu.VMEM((LANES, L), F32),
            pltpu.VMEM((L, LANES), BF16),
            pltpu.VMEM((GROUPS, STATE, GROUP_WIDTH), F32),
        ],
        compiler_params=pltpu.CompilerParams(
            dimension_semantics=("arbitrary", "arbitrary"), vmem_limit_bytes=VMEM_LIMIT_BYTES),
        name="ssd_shortconv_mixer",
    )(*([x] * X_COL_BLOCKS), mod, *consts)


def _merge_kernel(x_ref, mod_ref, ynorm_ref, ybin_ref, wgate_ref, bgate_ref, wa_ref, wb_ref,
                  wout_ref, lng_ref, lnb_ref, o_ref):
    x = x_ref[...]
    shift = mod_ref[:, 0:D_MODEL]
    scale = mod_ref[:, D_MODEL:2 * D_MODEL]
    gate = mod_ref[:, 2 * D_MODEL:3 * D_MODEL]
    u = (_layer_norm(x) * (1.0 + scale) + shift).astype(BF16)
    gates = jax.nn.sigmoid(_dot(u, wgate_ref[...]) + bgate_ref[...])
    y_a = _dot(ynorm_ref[0], wa_ref[0])
    for g in range(1, GROUPS):
        y_a = y_a + _dot(ynorm_ref[g], wa_ref[g])
    y_b = _dot(ybin_ref[...], wb_ref[...])
    merged = gates[:, 0:D_MODEL] * y_a + gates[:, D_MODEL:2 * D_MODEL] * y_b
    mixed = _dot(merged.astype(BF16), wout_ref[...])
    o_ref[...] = _layer_norm(DN_ALPHA * x + gate * mixed) * lng_ref[...] + lnb_ref[...]


def _merge(x, mod, ynorm, ybin, wgate, bgate, wa, wb, wout, lng, lnb):
    bsz, seq, _ = x.shape
    tm = min(MERGE_TILE, seq)
    assert seq % tm == 0
    consts = (wgate, bgate, wa, wb, wout, lng, lnb)
    return pl.pallas_call(
        _merge_kernel,
        grid=(bsz, seq // tm),
        in_specs=[
            pl.BlockSpec((None, tm, D_MODEL), lambda b, i: (b, i, 0)),
            pl.BlockSpec((None, 1, 3 * D_MODEL), lambda b, i: (b, 0, 0)),
            pl.BlockSpec((None, GROUPS, tm, GROUP_WIDTH), lambda b, i: (b, 0, i, 0)),
            pl.BlockSpec((None, tm, SC_WIDTH), lambda b, i: (b, i, 0)),
        ] + [_const(a.shape) for a in consts],
        out_specs=pl.BlockSpec((None, tm, D_MODEL), lambda b, i: (b, i, 0)),
        out_shape=jax.ShapeDtypeStruct((bsz, seq, D_MODEL), F32),
        compiler_params=pltpu.CompilerParams(
            dimension_semantics=("arbitrary", "arbitrary"), vmem_limit_bytes=VMEM_LIMIT_BYTES),
        name="gated_merge",
    )(x, mod, ynorm, ybin, *consts)


def _head_lanes(v):
    pad = jnp.zeros((LANES - HEAD_REPLICAS * HEADS,), F32)
    return jnp.concatenate([v] * HEAD_REPLICAS + [pad]).reshape(1, LANES)


def _selection_constants():
    L = CHUNK
    idx = jnp.arange(L)
    time = ROW_TILES * (idx % SUBLANES) + idx // SUBLANES
    causal = time[:, None] >= time[None, :]
    tri = causal.astype(BF16)
    negmask = jnp.where(causal, 0.0, -jnp.inf).astype(F32)
    j = jnp.arange(LANES)
    head_of_lane = jnp.where(j < HEAD_REPLICAS * HEADS, j % HEADS, -1)
    ch_head = jnp.arange(D_INNER) // HEAD_DIM
    eexp = (head_of_lane[:, None] == ch_head[None, :]).astype(BF16)
    eexp = eexp.reshape(LANES, GROUPS, GROUP_WIDTH).transpose(1, 0, 2)
    sel_head = jnp.arange(HEADS * LANES) // LANES
    esel = (head_of_lane[:, None] == sel_head[None, :]).astype(BF16)
    esel = esel.reshape(LANES, GROUPS, HEADS_PER_GROUP * LANES).transpose(1, 0, 2)
    return tri, negmask, esel, eexp


def kernel(x, c, w_mod, b_mod, w_in, conv_ssm_w, conv_ssm_b, dt_bias, a_log, d_skip, ssm_norm_w,
           conv_sc_w, w_branch_a, w_branch_b, w_gate, b_gate, w_out, ln_g, ln_b):
    bsz = x.shape[0]
    mod_all = _modulation(c, w_mod, b_mod)
    tri, negmask, esel, eexp = _selection_constants()
    for l in range(DEPTH):
        mod = mod_all[l].reshape(bsz, 1, 3 * D_MODEL)
        w = w_in[l]
        wz = w[:, :SPLIT_Z].astype(BF16).reshape(D_MODEL, GROUPS, GROUP_WIDTH).transpose(1, 0, 2)
        wxbc = w[:, SPLIT_Z:SPLIT_XBC].astype(BF16)
        wdt_h = w[:, SPLIT_XBC:SPLIT_DT].astype(BF16)
        wdt = jnp.concatenate(
            [wdt_h] * HEAD_REPLICAS + [jnp.zeros((D_MODEL, LANES - HEAD_REPLICAS * HEADS), BF16)], axis=1)
        wsc = w[:, SPLIT_DT:].astype(BF16)
        ynorm, ybin = _mixer(
            x, mod, wz, wxbc, wdt, wsc,
            conv_ssm_w[l], conv_ssm_b[l].reshape(1, XBC),
            _head_lanes(dt_bias[l]), _head_lanes(a_log[l]),
            jnp.repeat(d_skip[l], HEAD_DIM).reshape(GROUPS, 1, GROUP_WIDTH),
            ssm_norm_w[l].reshape(GROUPS, 1, GROUP_WIDTH),
            conv_sc_w[l], tri, negmask, esel, eexp)
        x = _merge(
            x, mod, ynorm, ybin,
            w_gate[l].astype(BF16), b_gate[l].reshape(1, 2 * D_MODEL),
            w_branch_a[l].astype(BF16).reshape(GROUPS, GROUP_WIDTH, D_MODEL),
            w_branch_b[l].astype(BF16), w_out[l].astype(BF16),
            ln_g[l].reshape(1, D_MODEL), ln_b[l].reshape(1, D_MODEL))
    return x
```

```python
import functools

import jax
import jax.numpy as jnp
from jax import lax
from jax.experimental import pallas as pl
from jax.experimental.pallas import tpu as pltpu

D_MODEL = 1024
DEPTH = 2
D_INNER = 2048
HEAD_DIM = 64
HEADS = 32
GROUPS = 8
STATE = 128
SSM_CONV = 4
CHUNK = 256
XBC = D_INNER + 2 * GROUPS * STATE
SC_WIDTH = 1024
SC_CONV = 3
HEADS_PER_GROUP = HEADS // GROUPS
GROUP_WIDTH = HEADS_PER_GROUP * HEAD_DIM
SPLIT_Z = D_INNER
SPLIT_XBC = SPLIT_Z + XBC
SPLIT_DT = SPLIT_XBC + HEADS
DN_ALPHA = (2 * DEPTH) ** 0.25
LN_EPS = 1e-5
RMS_EPS = 1e-5

LANES = 128
SUBLANES = 8
ROW_TILES = CHUNK // SUBLANES
SSM_HALO = SUBLANES * (SSM_CONV - 1)
SC_HALO = SUBLANES * (SC_CONV - 1)
X_COL_BLOCKS = D_MODEL // LANES
COL_BLOCK = 256
HEAD_REPLICAS = 3
MERGE_TILE = 512
VMEM_LIMIT_BYTES = 56 * 1024 * 1024

F32 = jnp.float32
BF16 = jnp.bfloat16


def _dot(a, b):
    return jnp.dot(a, b, preferred_element_type=F32)


def _layer_norm(x):
    mu = jnp.mean(x, axis=-1, keepdims=True)
    xc = x - mu
    var = jnp.mean(xc * xc, axis=-1, keepdims=True)
    return xc * lax.rsqrt(var + LN_EPS)


def _silu(x):
    h = 0.5 * x
    return h + h * jnp.tanh(h)


def _softplus(x):
    return jnp.maximum(x, 0.0) + jnp.log1p(jnp.exp(-jnp.abs(x)))


def _split3(x):
    hi = x.astype(BF16).astype(F32)
    r = x - hi
    mid = r.astype(BF16).astype(F32)
    return hi, mid, r - mid


def _lane_parts(x):
    hi, mid, lo = _split3(x)
    lane = lax.broadcasted_iota(jnp.int32, x.shape, 1)
    return jnp.where(lane < HEADS, hi, jnp.where(lane < 2 * HEADS, mid, lo)).astype(BF16)


def _load_time_interleaved(col_refs):
    return jnp.concatenate(
        [jnp.concatenate([ref[pl.ds(r, SUBLANES, stride=ROW_TILES), :] for r in range(ROW_TILES)],
                         axis=0) for ref in col_refs], axis=1)


def _store_time_ordered(stage_ref, value, out_ref, cols):
    for j in range(value.shape[1] // LANES):
        stage_ref[j] = value[:, j * LANES:(j + 1) * LANES]
    out_ref[:, cols] = jnp.concatenate(
        [jnp.concatenate([stage_ref.at[j][pl.ds(s, ROW_TILES, stride=SUBLANES), :]
                          for s in range(SUBLANES)], axis=0)
         for j in range(value.shape[1] // LANES)], axis=1).astype(out_ref.dtype)


def _conv_halo(prev_tail, cur_tail):
    rows = cur_tail.shape[0]
    sub = lax.broadcasted_iota(jnp.int32, cur_tail.shape, 0) % SUBLANES
    return jnp.where(sub == 0,
                     pltpu.roll(prev_tail, rows - (SUBLANES - 1), axis=0),
                     pltpu.roll(cur_tail, 1, axis=0))


def _causal_conv(buf, raw, tail, w, taps):
    halo = SUBLANES * (taps - 1)
    cur_tail = raw[CHUNK - halo:CHUNK, :]
    buf[0:halo, :] = _conv_halo(tail[...], cur_tail)
    buf[halo:halo + CHUNK, :] = raw
    tail[...] = cur_tail
    acc = w[taps - 1:taps, :] * raw
    for k in range(taps - 1):
        acc = acc + w[k:k + 1, :] * buf[pl.ds(SUBLANES * k, CHUNK), :]
    return acc


def _mod_kernel(c_ref, w_ref, b_ref, o_ref):
    o_ref[...] = jnp.dot(c_ref[...], w_ref[...], precision=lax.Precision.HIGHEST,
                         preferred_element_type=F32) + b_ref[...]


def _modulation(c, w_mod, b_mod):
    bsz = c.shape[0]
    rows = -(-bsz // SUBLANES) * SUBLANES
    c_pad = jnp.zeros((rows, D_MODEL), F32).at[:bsz].set(c)
    out = pl.pallas_call(
        _mod_kernel,
        grid=(DEPTH,),
        in_specs=[
            pl.BlockSpec((rows, D_MODEL), lambda l: (0, 0)),
            pl.BlockSpec((None, D_MODEL, 3 * D_MODEL), lambda l: (l, 0, 0)),
            pl.BlockSpec((None, 1, 3 * D_MODEL), lambda l: (l, 0, 0)),
        ],
        out_specs=pl.BlockSpec((None, rows, 3 * D_MODEL), lambda l: (l, 0, 0)),
        out_shape=jax.ShapeDtypeStruct((DEPTH, rows, 3 * D_MODEL), F32),
        compiler_params=pltpu.CompilerParams(
            dimension_semantics=("arbitrary",), vmem_limit_bytes=VMEM_LIMIT_BYTES),
        name="adaln_modulation",
    )(c_pad, w_mod, b_mod.reshape(DEPTH, 1, 3 * D_MODEL))
    return out[:, :bsz]


def _mixer_kernel(*refs):
    x_cols = refs[:X_COL_BLOCKS]
    (mod_ref, wx_ref, wbc_ref, wz_ref, wdt_ref, wsc_ref, convx_w_ref, convx_b_ref, convbc_w_ref,
     convbc_b_ref, dtb_ref, alog_ref, dskip_ref, normw_ref, convsc_ref, tri_ref, negmask_ref,
     esel_ref, eexp_ref,
     ynorm_ref, ybin_ref,
     u_ref, tailx_ref, tailbc_ref, sctail_ref, tmp_ref, yout_ref, actx_ref, actbc_ref, xs_ref,
     xdtm_ref, xw_ref, m_ref, bt_ref, c_ref, dt3_ref, te3_ref, acum3_ref, acumt_ref, e3_ref,
     state_ref) = refs[X_COL_BLOCKS:]
    L = CHUNK

    @pl.when(pl.program_id(1) == 0)
    def _():
        tailx_ref[...] = jnp.zeros_like(tailx_ref)
        tailbc_ref[...] = jnp.zeros_like(tailbc_ref)
        sctail_ref[...] = jnp.zeros_like(sctail_ref)
        state_ref[...] = jnp.zeros_like(state_ref)

    shift = mod_ref[:, 0:D_MODEL]
    scale = mod_ref[:, D_MODEL:2 * D_MODEL]
    u_ref[...] = (_layer_norm(_load_time_interleaved(x_cols)) * (1.0 + scale) + shift).astype(BF16)

    dt = _softplus(_dot(u_ref[...], wdt_ref[...]) + dtb_ref[...])
    a = dt * (-jnp.exp(alog_ref[...]))
    a_hi, a_mid, a_lo = _split3(a)
    tri = tri_ref[...]
    a_cum = (_dot(tri, a_hi.astype(BF16)) + _dot(tri, a_mid.astype(BF16))
             + _dot(tri, a_lo.astype(BF16)))
    acum3_ref[...] = _lane_parts(a_cum)
    acumt_ref[...] = a_cum.T
    e3_ref[...] = _lane_parts(jnp.exp(a_cum))
    dt3_ref[...] = _lane_parts(dt)
    te3_ref[...] = _lane_parts(jnp.exp(a_cum[L - 1:L, :] - a_cum))

    for blk in range(SC_WIDTH // COL_BLOCK):
        cols = slice(blk * COL_BLOCK, (blk + 1) * COL_BLOCK)
        u = u_ref[...]
        sc_b = _dot(u, wsc_ref[:, blk * COL_BLOCK:(blk + 1) * COL_BLOCK])
        sc_c = _dot(u, wsc_ref[:, SC_WIDTH + blk * COL_BLOCK:SC_WIDTH + (blk + 1) * COL_BLOCK])
        sc_x = _dot(u, wsc_ref[:, 2 * SC_WIDTH + blk * COL_BLOCK:2 * SC_WIDTH + (blk + 1) * COL_BLOCK])
        sc_g = _dot(u, wsc_ref[:, 3 * SC_WIDTH + blk * COL_BLOCK:3 * SC_WIDTH + (blk + 1) * COL_BLOCK])
        v = _causal_conv(tmp_ref.at[blk % 2], sc_c * sc_x, sctail_ref.at[blk], convsc_ref.at[blk],
                         SC_CONV)
        _store_time_ordered(yout_ref.at[blk % 2], _silu(sc_g) * sc_b * v, ybin_ref, cols)

    lane = lax.broadcasted_iota(jnp.int32, (1, COL_BLOCK), 1)

    def stage_a(p, slot):
        raw = _dot(u_ref[...], wx_ref[p])
        actx_ref[slot] = _silu(_causal_conv(tmp_ref.at[2 * slot], raw, tailx_ref.at[p],
                                            convx_w_ref.at[p], SSM_CONV) + convx_b_ref[p])
        raw = _dot(u_ref[...], wbc_ref[p])
        actbc_ref[slot] = _silu(_causal_conv(tmp_ref.at[2 * slot + 1], raw, tailbc_ref.at[p],
                                             convbc_w_ref.at[p], SSM_CONV) + convbc_b_ref[p])

    def stage_b(p, src, slot):
        act = actx_ref[src]
        xs_ref[slot] = act
        xdt = act * _dot(dt3_ref[...], eexp_ref[p])
        for hh in range(HEADS_PER_GROUP):
            in_head = (lane >= hh * HEAD_DIM) & (lane < (hh + 1) * HEAD_DIM)
            xdtm_ref[slot, hh * L:(hh + 1) * L, :] = jnp.where(in_head, xdt, 0.0).astype(BF16)
        xw_ref[slot] = (xdt * _dot(te3_ref[...], eexp_ref[p])).astype(BF16)
        act = actbc_ref[src]
        bmat = act[:, 0:STATE].astype(BF16)
        cmat = act[:, STATE:2 * STATE].astype(BF16)
        bt_ref[slot] = act[:, 0:STATE].T.astype(BF16)
        c_ref[slot] = cmat
        cb = lax.dot_general(cmat, bmat, (((1,), (1,)), ((), ())),
                             preferred_element_type=F32)
        acol4 = _dot(acum3_ref[...], esel_ref[p])
        for hh in range(HEADS_PER_GROUP):
            arow = acumt_ref[pl.ds(p * HEADS_PER_GROUP + hh, 1), :]
            acol = acol4[:, hh * LANES:(hh + 1) * LANES]
            seg = jnp.concatenate([acol, acol], axis=1) - arow + negmask_ref[...]
            m_ref[slot, :, hh * L:(hh + 1) * L] = (cb * jnp.exp(seg)).astype(BF16)

    def stage_c(g, slot):
        y = _dot(m_ref[slot], xdtm_ref[slot])
        state = state_ref[g]
        escale = _dot(e3_ref[...], eexp_ref[g])
        y = y + _dot(c_ref[slot], state.astype(BF16)) * escale
        state_ref[g] = state * escale[L - 1:L, :] + _dot(bt_ref[slot], xw_ref[slot])
        y = y + dskip_ref[g] * xs_ref[slot]
        y = y * _silu(_dot(u_ref[...], wz_ref[g]))
        y = y * lax.rsqrt(jnp.mean(y * y, axis=-1, keepdims=True) + RMS_EPS)
        _store_time_ordered(yout_ref.at[slot], y * normw_ref[g], ynorm_ref.at[g], slice(None))

    stage_a(0, 0)
    stage_a(1, 1)
    stage_b(0, 0, 0)

    def pair_body(i, carry):
        g = 2 * i
        stage_c(g, 0)
        stage_b(g + 1, 1, 1)
        stage_a(g + 2, 0)
        stage_c(g + 1, 1)
        stage_b(g + 2, 0, 0)
        stage_a(g + 3, 1)
        return carry

    lax.fori_loop(0, GROUPS // 2 - 1, pair_body, 0)
    stage_c(GROUPS - 2, 0)
    stage_b(GROUPS - 1, 1, 1)
    stage_c(GROUPS - 1, 1)


def _const(shape):
    return pl.BlockSpec(shape, lambda b, i: (0,) * len(shape), pipeline_mode=pl.Buffered(1))


def _mixer(x, mod, *consts):
    bsz, seq, _ = x.shape
    L = CHUNK
    assert seq % L == 0
    return pl.pallas_call(
        _mixer_kernel,
        grid=(bsz, seq // L),
        in_specs=[
            pl.BlockSpec((None, L, LANES), functools.partial(lambda j, b, i: (b, i, j), j))
            for j in range(X_COL_BLOCKS)
        ] + [
            pl.BlockSpec((None, 1, 3 * D_MODEL), lambda b, i: (b, 0, 0)),
        ] + [_const(a.shape) for a in consts],
        out_specs=[
            pl.BlockSpec((None, GROUPS, L, GROUP_WIDTH), lambda b, i: (b, 0, i, 0)),
            pl.BlockSpec((None, L, SC_WIDTH), lambda b, i: (b, i, 0)),
        ],
        out_shape=[
            jax.ShapeDtypeStruct((bsz, GROUPS, seq, GROUP_WIDTH), BF16),
            jax.ShapeDtypeStruct((bsz, seq, SC_WIDTH), BF16),
        ],
        scratch_shapes=[
            pltpu.VMEM((L, D_MODEL), BF16),
            pltpu.VMEM((GROUPS, SSM_HALO, GROUP_WIDTH), F32),
            pltpu.VMEM((GROUPS, SSM_HALO, 2 * STATE), F32),
            pltpu.VMEM((SC_WIDTH // COL_BLOCK, SC_HALO, COL_BLOCK), F32),
            pltpu.VMEM((4, SSM_HALO + L, COL_BLOCK), F32),
            pltpu.VMEM((2, COL_BLOCK // LANES, L, LANES), F32),
            pltpu.VMEM((2, L, GROUP_WIDTH), F32),
            pltpu.VMEM((2, L, 2 * STATE), F32),
            pltpu.VMEM((2, L, GROUP_WIDTH), F32),
            pltpu.VMEM((2, HEADS_PER_GROUP * L, GROUP_WIDTH), BF16),
            pltpu.VMEM((2, L, GROUP_WIDTH), BF16),
            pltpu.VMEM((2, L, HEADS_PER_GROUP * L), BF16),
            pltpu.VMEM((2, STATE, L), BF16),
            pltpu.VMEM((2, L, STATE), BF16),
            pltpu.VMEM((L, LANES), BF16),
            pltpu.VMEM((L, LANES), BF16),
            pltpu.VMEM((L, LANES), BF16),
            pltpu.VMEM((LANES, L), F32),
            pltpu.VMEM((L, LANES), BF16),
            pltpu.VMEM((GROUPS, STATE, GROUP_WIDTH), F32),
        ],
        compiler_params=pltpu.CompilerParams(
            dimension_semantics=("arbitrary", "arbitrary"), vmem_limit_bytes=VMEM_LIMIT_BYTES),
        name="ssd_shortconv_mixer",
    )(*([x] * X_COL_BLOCKS), mod, *consts)


def _merge_kernel(x_ref, mod_ref, ynorm_ref, ybin_ref, wgate_ref, bgate_ref, wa_ref, wb_ref,
                  wout_ref, lng_ref, lnb_ref, o_ref):
    x = x_ref[...]
    shift = mod_ref[:, 0:D_MODEL]
    scale = mod_ref[:, D_MODEL:2 * D_MODEL]
    gate = mod_ref[:, 2 * D_MODEL:3 * D_MODEL]
    u = (_layer_norm(x) * (1.0 + scale) + shift).astype(BF16)
    gates = jax.nn.sigmoid(_dot(u, wgate_ref[...]) + bgate_ref[...])
    y_a = _dot(ynorm_ref[0], wa_ref[0])
    for g in range(1, GROUPS):
        y_a = y_a + _dot(ynorm_ref[g], wa_ref[g])
    y_b = _dot(ybin_ref[...], wb_ref[...])
    merged = gates[:, 0:D_MODEL] * y_a + gates[:, D_MODEL:2 * D_MODEL] * y_b
    mixed = _dot(merged.astype(BF16), wout_ref[...])
    o_ref[...] = _layer_norm(DN_ALPHA * x + gate * mixed) * lng_ref[...] + lnb_ref[...]


def _merge(x, mod, ynorm, ybin, wgate, bgate, wa, wb, wout, lng, lnb):
    bsz, seq, _ = x.shape
    tm = min(MERGE_TILE, seq)
    assert seq % tm == 0
    consts = (wgate, bgate, wa, wb, wout, lng, lnb)
    return pl.pallas_call(
        _merge_kernel,
        grid=(bsz, seq // tm),
        in_specs=[
            pl.BlockSpec((None, tm, D_MODEL), lambda b, i: (b, i, 0)),
            pl.BlockSpec((None, 1, 3 * D_MODEL), lambda b, i: (b, 0, 0)),
            pl.BlockSpec((None, GROUPS, tm, GROUP_WIDTH), lambda b, i: (b, 0, i, 0)),
            pl.BlockSpec((None, tm, SC_WIDTH), lambda b, i: (b, i, 0)),
        ] + [_const(a.shape) for a in consts],
        out_specs=pl.BlockSpec((None, tm, D_MODEL), lambda b, i: (b, i, 0)),
        out_shape=jax.ShapeDtypeStruct((bsz, seq, D_MODEL), F32),
        compiler_params=pltpu.CompilerParams(
            dimension_semantics=("arbitrary", "arbitrary"), vmem_limit_bytes=VMEM_LIMIT_BYTES),
        name="gated_merge",
    )(x, mod, ynorm, ybin, *consts)


def _head_lanes(v):
    pad = jnp.zeros((LANES - HEAD_REPLICAS * HEADS,), F32)
    return jnp.concatenate([v] * HEAD_REPLICAS + [pad]).reshape(1, LANES)


def _selection_constants():
    L = CHUNK
    idx = jnp.arange(L)
    time = ROW_TILES * (idx % SUBLANES) + idx // SUBLANES
    causal = time[:, None] >= time[None, :]
    tri = causal.astype(BF16)
    negmask = jnp.where(causal, 0.0, -jnp.inf).astype(F32)
    j = jnp.arange(LANES)
    head_of_lane = jnp.where(j < HEAD_REPLICAS * HEADS, j % HEADS, -1)
    ch_head = jnp.arange(D_INNER) // HEAD_DIM
    eexp = (head_of_lane[:, None] == ch_head[None, :]).astype(BF16)
    eexp = eexp.reshape(LANES, GROUPS, GROUP_WIDTH).transpose(1, 0, 2)
    sel_head = jnp.arange(HEADS * LANES) // LANES
    esel = (head_of_lane[:, None] == sel_head[None, :]).astype(BF16)
    esel = esel.reshape(LANES, GROUPS, HEADS_PER_GROUP * LANES).transpose(1, 0, 2)
    return tri, negmask, esel, eexp


def _by_group(a, width):
    return a.reshape(a.shape[0], GROUPS, width).transpose(1, 0, 2)


def _bc_by_group(a):
    b = a[:, D_INNER:D_INNER + GROUPS * STATE].reshape(a.shape[0], GROUPS, STATE)
    c = a[:, D_INNER + GROUPS * STATE:].reshape(a.shape[0], GROUPS, STATE)
    return jnp.concatenate([b, c], axis=2).transpose(1, 0, 2)


def kernel(x, c, w_mod, b_mod, w_in, conv_ssm_w, conv_ssm_b, dt_bias, a_log, d_skip, ssm_norm_w,
           conv_sc_w, w_branch_a, w_branch_b, w_gate, b_gate, w_out, ln_g, ln_b):
    bsz = x.shape[0]
    mod_all = _modulation(c, w_mod, b_mod)
    tri, negmask, esel, eexp = _selection_constants()
    for l in range(DEPTH):
        mod = mod_all[l].reshape(bsz, 1, 3 * D_MODEL)
        w = w_in[l]
        wxbc = w[:, SPLIT_Z:SPLIT_XBC].astype(BF16)
        wdt_h = w[:, SPLIT_XBC:SPLIT_DT].astype(BF16)
        wdt = jnp.concatenate(
            [wdt_h] * HEAD_REPLICAS + [jnp.zeros((D_MODEL, LANES - HEAD_REPLICAS * HEADS), BF16)], axis=1)
        convw, convb = conv_ssm_w[l], conv_ssm_b[l].reshape(1, XBC)
        ynorm, ybin = _mixer(
            x, mod,
            _by_group(wxbc[:, :D_INNER], GROUP_WIDTH), _bc_by_group(wxbc),
            _by_group(w[:, :SPLIT_Z].astype(BF16), GROUP_WIDTH), wdt, w[:, SPLIT_DT:].astype(BF16),
            _by_group(convw[:, :D_INNER], GROUP_WIDTH), _by_group(convb[:, :D_INNER], GROUP_WIDTH),
            _bc_by_group(convw), _bc_by_group(convb),
            _head_lanes(dt_bias[l]), _head_lanes(a_log[l]),
            jnp.repeat(d_skip[l], HEAD_DIM).reshape(GROUPS, 1, GROUP_WIDTH),
            ssm_norm_w[l].reshape(GROUPS, 1, GROUP_WIDTH),
            conv_sc_w[l].reshape(SC_CONV, SC_WIDTH // COL_BLOCK, COL_BLOCK).transpose(1, 0, 2),
            tri, negmask, esel, eexp)
        x = _merge(
            x, mod, ynorm, ybin,
            w_gate[l].astype(BF16), b_gate[l].reshape(1, 2 * D_MODEL),
            w_branch_a[l].astype(BF16).reshape(GROUPS, GROUP_WIDTH, D_MODEL),
            w_branch_b[l].astype(BF16), w_out[l].astype(BF16),
            ln_g[l].reshape(1, D_MODEL), ln_b[l].reshape(1, D_MODEL))
    return x
```

```python
import functools

import jax
import jax.numpy as jnp
from jax import lax
from jax.experimental import pallas as pl
from jax.experimental.pallas import tpu as pltpu

D_MODEL = 1024
DEPTH = 2
D_INNER = 2048
HEAD_DIM = 64
HEADS = 32
GROUPS = 8
STATE = 128
SSM_CONV = 4
CHUNK = 256
XBC = D_INNER + 2 * GROUPS * STATE
SC_WIDTH = 1024
SC_CONV = 3
HEADS_PER_GROUP = HEADS // GROUPS
GROUP_WIDTH = HEADS_PER_GROUP * HEAD_DIM
SPLIT_Z = D_INNER
SPLIT_XBC = SPLIT_Z + XBC
SPLIT_DT = SPLIT_XBC + HEADS
DN_ALPHA = (2 * DEPTH) ** 0.25
LN_EPS = 1e-5
RMS_EPS = 1e-5
LOG2_E = 1.4426950408889634

LANES = 128
SUBLANES = 8
ROW_TILES = CHUNK // SUBLANES
SSM_HALO = SUBLANES * (SSM_CONV - 1)
SC_HALO = SUBLANES * (SC_CONV - 1)
COL_BLOCK = 256
HEAD_REPLICAS = 3
MERGE_TILE = 512
VMEM_LIMIT_BYTES = 56 * 1024 * 1024

F32 = jnp.float32
BF16 = jnp.bfloat16


def _dot(a, b):
    return jnp.dot(a, b, preferred_element_type=F32)


def _layer_norm(x):
    mu = jnp.mean(x, axis=-1, keepdims=True)
    xc = x - mu
    var = jnp.mean(xc * xc, axis=-1, keepdims=True)
    return xc * lax.rsqrt(var + LN_EPS)


def _silu(x):
    h = 0.5 * x
    return h + h * jnp.tanh(h)


def _softplus(x):
    return jnp.maximum(x, 0.0) + jnp.log1p(jnp.exp(-jnp.abs(x)))


def _split3(x):
    hi = x.astype(BF16).astype(F32)
    r = x - hi
    mid = r.astype(BF16).astype(F32)
    return hi, mid, r - mid


def _lane_parts(x):
    hi, mid, lo = _split3(x)
    lane = lax.broadcasted_iota(jnp.int32, x.shape, 1)
    return jnp.where(lane < HEADS, hi, jnp.where(lane < 2 * HEADS, mid, lo)).astype(BF16)


def _store_time_ordered(stage_ref, value, out_ref, cols):
    for j in range(value.shape[1] // LANES):
        stage_ref[j] = value[:, j * LANES:(j + 1) * LANES]
    out_ref[:, cols] = jnp.concatenate(
        [jnp.concatenate([stage_ref.at[j][pl.ds(s, ROW_TILES, stride=SUBLANES), :]
                          for s in range(SUBLANES)], axis=0)
         for j in range(value.shape[1] // LANES)], axis=1).astype(out_ref.dtype)


def _conv_halo(prev_tail, cur_tail):
    rows = cur_tail.shape[0]
    sub = lax.broadcasted_iota(jnp.int32, cur_tail.shape, 0) % SUBLANES
    return jnp.where(sub == 0,
                     pltpu.roll(prev_tail, rows - (SUBLANES - 1), axis=0),
                     pltpu.roll(cur_tail, 1, axis=0))


def _causal_conv(buf, raw, tail, w, taps, fresh):
    halo = SUBLANES * (taps - 1)
    cur_tail = raw[CHUNK - halo:CHUNK, :]
    buf[0:halo, :] = _conv_halo(jnp.where(fresh, 0.0, tail[...]), cur_tail)
    buf[halo:halo + CHUNK, :] = raw
    tail[...] = cur_tail
    acc = w[taps - 1:taps, :] * raw
    for k in range(taps - 1):
        acc = acc + w[k:k + 1, :] * buf[pl.ds(SUBLANES * k, CHUNK), :]
    return acc


def _mod_kernel(c_ref, w_ref, b_ref, o_ref):
    o_ref[...] = jnp.dot(c_ref[...], w_ref[...], precision=lax.Precision.HIGHEST,
                         preferred_element_type=F32) + b_ref[...]


def _modulation(c, w_mod, b_mod):
    bsz = c.shape[0]
    rows = -(-bsz // SUBLANES) * SUBLANES
    c_pad = jnp.zeros((rows, D_MODEL), F32).at[:bsz].set(c)
    out = pl.pallas_call(
        _mod_kernel,
        grid=(DEPTH,),
        in_specs=[
            pl.BlockSpec((rows, D_MODEL), lambda l: (0, 0)),
            pl.BlockSpec((None, D_MODEL, 3 * D_MODEL), lambda l: (l, 0, 0)),
            pl.BlockSpec((None, 1, 3 * D_MODEL), lambda l: (l, 0, 0)),
        ],
        out_specs=pl.BlockSpec((None, rows, 3 * D_MODEL), lambda l: (l, 0, 0)),
        out_shape=jax.ShapeDtypeStruct((DEPTH, rows, 3 * D_MODEL), F32),
        compiler_params=pltpu.CompilerParams(
            dimension_semantics=("arbitrary",), vmem_limit_bytes=VMEM_LIMIT_BYTES),
        name="adaln_modulation",
    )(c_pad, w_mod, b_mod.reshape(DEPTH, 1, 3 * D_MODEL))
    return out[:, :bsz]


def _mixer_kernel(n_batch, n_chunks, *refs):
    (x_cur, x_nxt,
     mod_ref, wx_ref, wbc_ref, wz_ref, wdt_ref, wsc_ref, convx_w_ref, convx_b_ref, convbc_w_ref,
     convbc_b_ref, dtb_ref, alog_ref, dskip_ref, normw_ref, convsc_ref, tri_ref, negmask_ref,
     esel_ref, eexp_ref, perm_ref,
     ynorm_ref, ybin_ref,
     u_ref, un_ref, tailx_ref, tailbc_ref, sctail_ref, tmp_ref, yout_ref, actx_ref, actbc_ref,
     xs_ref, xdtm_ref, xw_ref, m_ref, bt_ref, c_ref, hb_ref, hbn_ref, at_ref, atn_ref,
     state_ref) = refs
    L = CHUNK
    DT3, TE3, ACUM3, E3 = range(4)
    batch = pl.program_id(0)
    chunk = pl.program_id(1)
    seq_starts = chunk == 0
    last_chunk = chunk == n_chunks - 1
    nxt_batch = jnp.minimum(batch + last_chunk.astype(jnp.int32), n_batch - 1)
    lane = lax.broadcasted_iota(jnp.int32, (1, COL_BLOCK), 1)

    def norm_and_heads(x_ref, b):
        mod = mod_ref[b]
        shift = mod[:, 0:D_MODEL]
        scale = mod[:, D_MODEL:2 * D_MODEL]
        u = (_layer_norm(x_ref[...]) * (1.0 + scale) + shift).astype(BF16)
        un_ref[...] = _dot(perm_ref[...], u).astype(BF16)
        dt = _softplus(_dot(un_ref[...], wdt_ref[...]) + dtb_ref[...])
        a = dt * (-jnp.exp(alog_ref[...]))
        a_hi, a_mid, a_lo = _split3(a)
        tri = tri_ref[...]
        a_cum = (_dot(tri, a_hi.astype(BF16)) + _dot(tri, a_mid.astype(BF16))
                 + _dot(tri, a_lo.astype(BF16)))
        hbn_ref[DT3] = _lane_parts(dt)
        hbn_ref[TE3] = _lane_parts(jnp.exp(a_cum[L - 1:L, :] - a_cum))
        hbn_ref[E3] = _lane_parts(jnp.exp(a_cum))
        a_cum2 = a_cum * LOG2_E
        hbn_ref[ACUM3] = _lane_parts(a_cum2)
        atn_ref[...] = a_cum2.T

    def stage_a(p, slot, u, fresh):
        raw = _dot(u[...], wx_ref[p])
        actx_ref[slot] = _silu(_causal_conv(tmp_ref.at[2 * slot], raw, tailx_ref.at[p],
                                            convx_w_ref.at[p], SSM_CONV, fresh) + convx_b_ref[p])
        raw = _dot(u[...], wbc_ref[p])
        actbc_ref[slot] = _silu(_causal_conv(tmp_ref.at[2 * slot + 1], raw, tailbc_ref.at[p],
                                             convbc_w_ref.at[p], SSM_CONV, fresh) + convbc_b_ref[p])

    def stage_b(p, src, slot, hb, at):
        act = actx_ref[src]
        xs_ref[slot] = act
        xdt = act * _dot(hb[DT3], eexp_ref[p])
        for hh in range(HEADS_PER_GROUP):
            in_head = (lane >= hh * HEAD_DIM) & (lane < (hh + 1) * HEAD_DIM)
            xdtm_ref[slot, hh * L:(hh + 1) * L, :] = jnp.where(in_head, xdt, 0.0).astype(BF16)
        xw_ref[slot] = (xdt * _dot(hb[TE3], eexp_ref[p])).astype(BF16)
        act = actbc_ref[src]
        bmat = act[:, 0:STATE].astype(BF16)
        cmat = act[:, STATE:2 * STATE].astype(BF16)
        bt_ref[slot] = act[:, 0:STATE].T.astype(BF16)
        c_ref[slot] = cmat
        cb = lax.dot_general(cmat, bmat, (((1,), (1,)), ((), ())),
                             preferred_element_type=F32)
        acol4 = _dot(hb[ACUM3], esel_ref[p])
        for hh in range(HEADS_PER_GROUP):
            arow = at[pl.ds(p * HEADS_PER_GROUP + hh, 1), :]
            acol = acol4[:, hh * LANES:(hh + 1) * LANES]
            seg = jnp.concatenate([acol, acol], axis=1) - arow + negmask_ref[...]
            m_ref[slot, :, hh * L:(hh + 1) * L] = (cb * jnp.exp2(seg)).astype(BF16)

    def stage_c(g, slot):
        y = _dot(m_ref[slot], xdtm_ref[slot])
        state = jnp.where(seq_starts, 0.0, state_ref[g])
        escale = _dot(hb_ref[E3], eexp_ref[g])
        y = y + _dot(c_ref[slot], state.astype(BF16)) * escale
        state_ref[g] = state * escale[L - 1:L, :] + _dot(bt_ref[slot], xw_ref[slot])
        y = y + dskip_ref[g] * xs_ref[slot]
        y = y * _silu(_dot(u_ref[...], wz_ref[g]))
        y = y * lax.rsqrt(jnp.mean(y * y, axis=-1, keepdims=True) + RMS_EPS)
        _store_time_ordered(yout_ref.at[slot], y * normw_ref[g], ynorm_ref.at[g], slice(None))

    @pl.when((batch == 0) & (chunk == 0))
    def _():
        tailx_ref[...] = jnp.zeros_like(tailx_ref)
        tailbc_ref[...] = jnp.zeros_like(tailbc_ref)
        sctail_ref[...] = jnp.zeros_like(sctail_ref)
        state_ref[...] = jnp.zeros_like(state_ref)
        norm_and_heads(x_cur, batch)
        stage_a(0, 0, un_ref, True)
        stage_a(1, 1, un_ref, True)
        stage_b(0, 0, 0, hbn_ref, atn_ref)

    u_ref[...] = un_ref[...]
    hb_ref[...] = hbn_ref[...]
    at_ref[...] = atn_ref[...]

    def pair_body(i, carry):
        g = 2 * i
        stage_c(g, 0)
        stage_b(g + 1, 1, 1, hb_ref, at_ref)
        stage_a(g + 2, 0, u_ref, seq_starts)
        stage_c(g + 1, 1)
        stage_b(g + 2, 0, 0, hb_ref, at_ref)
        stage_a(g + 3, 1, u_ref, seq_starts)
        return carry

    lax.fori_loop(0, GROUPS // 2 - 1, pair_body, 0)
    stage_c(GROUPS - 2, 0)
    stage_b(GROUPS - 1, 1, 1, hb_ref, at_ref)
    norm_and_heads(x_nxt, nxt_batch)
    stage_a(0, 0, un_ref, last_chunk)
    stage_c(GROUPS - 1, 1)
    stage_b(0, 0, 0, hbn_ref, atn_ref)
    stage_a(1, 1, un_ref, last_chunk)

    for blk in range(SC_WIDTH // COL_BLOCK):
        cols = slice(blk * COL_BLOCK, (blk + 1) * COL_BLOCK)
        u = u_ref[...]
        sc_b = _dot(u, wsc_ref[:, blk * COL_BLOCK:(blk + 1) * COL_BLOCK])
        sc_c = _dot(u, wsc_ref[:, SC_WIDTH + blk * COL_BLOCK:SC_WIDTH + (blk + 1) * COL_BLOCK])
        sc_x = _dot(u, wsc_ref[:, 2 * SC_WIDTH + blk * COL_BLOCK:2 * SC_WIDTH + (blk + 1) * COL_BLOCK])
        sc_g = _dot(u, wsc_ref[:, 3 * SC_WIDTH + blk * COL_BLOCK:3 * SC_WIDTH + (blk + 1) * COL_BLOCK])
        v = _causal_conv(tmp_ref.at[4 + blk % 2], sc_c * sc_x, sctail_ref.at[blk], convsc_ref.at[blk],
                         SC_CONV, seq_starts)
        _store_time_ordered(yout_ref.at[2 + blk % 2], _silu(sc_g) * sc_b * v, ybin_ref, cols)


def _const(shape):
    return pl.BlockSpec(shape, lambda b, i: (0,) * len(shape), pipeline_mode=pl.Buffered(1))


def _mixer(x, mod, *consts):
    bsz, seq, _ = x.shape
    L = CHUNK
    assert seq % L == 0
    n_chunks = seq // L

    def next_block(b, i):
        wraps = (i == n_chunks - 1).astype(jnp.int32)
        return (jnp.minimum(b + wraps, bsz - 1), (i + 1) * (1 - wraps), 0)

    return pl.pallas_call(
        functools.partial(_mixer_kernel, bsz, n_chunks),
        grid=(bsz, n_chunks),
        in_specs=[
            pl.BlockSpec((None, L, D_MODEL), lambda b, i: (b, i, 0)),
            pl.BlockSpec((None, L, D_MODEL), next_block),
        ] + [_const(a.shape) for a in (mod,) + consts],
        out_specs=[
            pl.BlockSpec((None, GROUPS, L, GROUP_WIDTH), lambda b, i: (b, 0, i, 0)),
            pl.BlockSpec((None, L, SC_WIDTH), lambda b, i: (b, i, 0)),
        ],
        out_shape=[
            jax.ShapeDtypeStruct((bsz, GROUPS, seq, GROUP_WIDTH), BF16),
            jax.ShapeDtypeStruct((bsz, seq, SC_WIDTH), BF16),
        ],
        scratch_shapes=[
            pltpu.VMEM((L, D_MODEL), BF16),
            pltpu.VMEM((L, D_MODEL), BF16),
            pltpu.VMEM((GROUPS, SSM_HALO, GROUP_WIDTH), F32),
            pltpu.VMEM((GROUPS, SSM_HALO, 2 * STATE), F32),
            pltpu.VMEM((SC_WIDTH // COL_BLOCK, SC_HALO, COL_BLOCK), F32),
            pltpu.VMEM((6, SSM_HALO + L, COL_BLOCK), F32),
            pltpu.VMEM((4, COL_BLOCK // LANES, L, LANES), F32),
            pltpu.VMEM((2, L, GROUP_WIDTH), F32),
            pltpu.VMEM((2, L, 2 * STATE), F32),
            pltpu.VMEM((2, L, GROUP_WIDTH), F32),
            pltpu.VMEM((2, HEADS_PER_GROUP * L, GROUP_WIDTH), BF16),
            pltpu.VMEM((2, L, GROUP_WIDTH), BF16),
            pltpu.VMEM((2, L, HEADS_PER_GROUP * L), BF16),
            pltpu.VMEM((2, STATE, L), BF16),
            pltpu.VMEM((2, L, STATE), BF16),
            pltpu.VMEM((4, L, LANES), BF16),
            pltpu.VMEM((4, L, LANES), BF16),
            pltpu.VMEM((LANES, L), F32),
            pltpu.VMEM((LANES, L), F32),
            pltpu.VMEM((GROUPS, STATE, GROUP_WIDTH), F32),
        ],
        compiler_params=pltpu.CompilerParams(
            dimension_semantics=("arbitrary", "arbitrary"), vmem_limit_bytes=VMEM_LIMIT_BYTES),
        name="ssd_shortconv_mixer",
    )(x, x, mod, *consts)


def _merge_kernel(x_ref, mod_ref, ynorm_ref, ybin_ref, wgate_ref, bgate_ref, wa_ref, wb_ref,
                  wout_ref, lng_ref, lnb_ref, o_ref):
    x = x_ref[...]
    shift = mod_ref[:, 0:D_MODEL]
    scale = mod_ref[:, D_MODEL:2 * D_MODEL]
    gate = mod_ref[:, 2 * D_MODEL:3 * D_MODEL]
    u = (_layer_norm(x) * (1.0 + scale) + shift).astype(BF16)
    gates = jax.nn.sigmoid(_dot(u, wgate_ref[...]) + bgate_ref[...])
    y_a = _dot(ynorm_ref[0], wa_ref[0])
    for g in range(1, GROUPS):
        y_a = y_a + _dot(ynorm_ref[g], wa_ref[g])
    y_b = _dot(ybin_ref[...], wb_ref[...])
    merged = gates[:, 0:D_MODEL] * y_a + gates[:, D_MODEL:2 * D_MODEL] * y_b
    mixed = _dot(merged.astype(BF16), wout_ref[...])
    o_ref[...] = _layer_norm(DN_ALPHA * x + gate * mixed) * lng_ref[...] + lnb_ref[...]


def _merge(x, mod, ynorm, ybin, wgate, bgate, wa, wb, wout, lng, lnb):
    bsz, seq, _ = x.shape
    tm = min(MERGE_TILE, seq)
    assert seq % tm == 0
    consts = (wgate, bgate, wa, wb, wout, lng, lnb)
    return pl.pallas_call(
        _merge_kernel,
        grid=(bsz, seq // tm),
        in_specs=[
            pl.BlockSpec((None, tm, D_MODEL), lambda b, i: (b, i, 0)),
            pl.BlockSpec((None, 1, 3 * D_MODEL), lambda b, i: (b, 0, 0)),
            pl.BlockSpec((None, GROUPS, tm, GROUP_WIDTH), lambda b, i: (b, 0, i, 0)),
            pl.BlockSpec((None, tm, SC_WIDTH), lambda b, i: (b, i, 0)),
        ] + [_const(a.shape) for a in consts],
        out_specs=pl.BlockSpec((None, tm, D_MODEL), lambda b, i: (b, i, 0)),
        out_shape=jax.ShapeDtypeStruct((bsz, seq, D_MODEL), F32),
        compiler_params=pltpu.CompilerParams(
            dimension_semantics=("arbitrary", "arbitrary"), vmem_limit_bytes=VMEM_LIMIT_BYTES),
        name="gated_merge",
    )(x, mod, ynorm, ybin, *consts)


def _head_lanes(v):
    pad = jnp.zeros((LANES - HEAD_REPLICAS * HEADS,), F32)
    return jnp.concatenate([v] * HEAD_REPLICAS + [pad]).reshape(1, LANES)


def _selection_constants():
    L = CHUNK
    idx = jnp.arange(L)
    time = ROW_TILES * (idx % SUBLANES) + idx // SUBLANES
    causal = time[:, None] >= time[None, :]
    tri = causal.astype(BF16)
    negmask = jnp.where(causal, 0.0, -jnp.inf).astype(F32)
    j = jnp.arange(LANES)
    head_of_lane = jnp.where(j < HEAD_REPLICAS * HEADS, j % HEADS, -1)
    ch_head = jnp.arange(D_INNER) // HEAD_DIM
    eexp = (head_of_lane[:, None] == ch_head[None, :]).astype(BF16)
    eexp = eexp.reshape(LANES, GROUPS, GROUP_WIDTH).transpose(1, 0, 2)
    sel_head = jnp.arange(HEADS * LANES) // LANES
    esel = (head_of_lane[:, None] == sel_head[None, :]).astype(BF16)
    esel = esel.reshape(LANES, GROUPS, HEADS_PER_GROUP * LANES).transpose(1, 0, 2)
    perm = (time[:, None] == idx[None, :]).astype(BF16)
    return tri, negmask, esel, eexp, perm


def _by_group(a, width):
    return a.reshape(a.shape[0], GROUPS, width).transpose(1, 0, 2)


def _bc_by_group(a):
    b = a[:, D_INNER:D_INNER + GROUPS * STATE].reshape(a.shape[0], GROUPS, STATE)
    c = a[:, D_INNER + GROUPS * STATE:].reshape(a.shape[0], GROUPS, STATE)
    return jnp.concatenate([b, c], axis=2).transpose(1, 0, 2)


def kernel(x, c, w_mod, b_mod, w_in, conv_ssm_w, conv_ssm_b, dt_bias, a_log, d_skip, ssm_norm_w,
           conv_sc_w, w_branch_a, w_branch_b, w_gate, b_gate, w_out, ln_g, ln_b):
    bsz = x.shape[0]
    mod_all = _modulation(c, w_mod, b_mod)
    tri, negmask, esel, eexp, perm = _selection_constants()
    for l in range(DEPTH):
        mod = mod_all[l].reshape(bsz, 1, 3 * D_MODEL)
        w = w_in[l]
        wxbc = w[:, SPLIT_Z:SPLIT_XBC].astype(BF16)
        wdt_h = w[:, SPLIT_XBC:SPLIT_DT].astype(BF16)
        wdt = jnp.concatenate(
            [wdt_h] * HEAD_REPLICAS + [jnp.zeros((D_MODEL, LANES - HEAD_REPLICAS * HEADS), BF16)], axis=1)
        convw, convb = conv_ssm_w[l], conv_ssm_b[l].reshape(1, XBC)
        ynorm, ybin = _mixer(
            x, mod,
            _by_group(wxbc[:, :D_INNER], GROUP_WIDTH), _bc_by_group(wxbc),
            _by_group(w[:, :SPLIT_Z].astype(BF16), GROUP_WIDTH), wdt, w[:, SPLIT_DT:].astype(BF16),
            _by_group(convw[:, :D_INNER], GROUP_WIDTH), _by_group(convb[:, :D_INNER], GROUP_WIDTH),
            _bc_by_group(convw), _bc_by_group(convb),
            _head_lanes(dt_bias[l]), _head_lanes(a_log[l]),
            jnp.repeat(d_skip[l], HEAD_DIM).reshape(GROUPS, 1, GROUP_WIDTH),
            ssm_norm_w[l].reshape(GROUPS, 1, GROUP_WIDTH),
            conv_sc_w[l].reshape(SC_CONV, SC_WIDTH // COL_BLOCK, COL_BLOCK).transpose(1, 0, 2),
            tri, negmask, esel, eexp, perm)
        x = _merge(
            x, mod, ynorm, ybin,
            w_gate[l].astype(BF16), b_gate[l].reshape(1, 2 * D_MODEL),
            w_branch_a[l].astype(BF16).reshape(GROUPS, GROUP_WIDTH, D_MODEL),
            w_branch_b[l].astype(BF16), w_out[l].astype(BF16),
            ln_g[l].reshape(1, D_MODEL), ln_b[l].reshape(1, D_MODEL))
    return x
```

```python
import functools

import jax
import jax.numpy as jnp
from jax import lax
from jax.experimental import pallas as pl
from jax.experimental.pallas import tpu as pltpu

D_MODEL = 1024
DEPTH = 2
D_INNER = 2048
HEAD_DIM = 64
HEADS = 32
GROUPS = 8
STATE = 128
SSM_CONV = 4
CHUNK = 256
XBC = D_INNER + 2 * GROUPS * STATE
SC_WIDTH = 1024
SC_CONV = 3
HEADS_PER_GROUP = HEADS // GROUPS
GROUP_WIDTH = HEADS_PER_GROUP * HEAD_DIM
SPLIT_Z = D_INNER
SPLIT_XBC = SPLIT_Z + XBC
SPLIT_DT = SPLIT_XBC + HEADS
DN_ALPHA = (2 * DEPTH) ** 0.25
LN_EPS = 1e-5
RMS_EPS = 1e-5
LOG2_E = 1.4426950408889634

LANES = 128
SUBLANES = 8
ROW_TILES = CHUNK // SUBLANES
SSM_HALO = SUBLANES * (SSM_CONV - 1)
SC_HALO = SUBLANES * (SC_CONV - 1)
COL_BLOCK = 256
HEAD_REPLICAS = 3
MERGE_TILE = 512
VMEM_LIMIT_BYTES = 56 * 1024 * 1024

F32 = jnp.float32
BF16 = jnp.bfloat16


def _dot(a, b):
    return jnp.dot(a, b, preferred_element_type=F32)


def _layer_norm(x):
    mu = jnp.mean(x, axis=-1, keepdims=True)
    xc = x - mu
    var = jnp.mean(xc * xc, axis=-1, keepdims=True)
    return xc * lax.rsqrt(var + LN_EPS)


def _silu(x):
    h = 0.5 * x
    return h + h * jnp.tanh(h)


def _softplus(x):
    return jnp.maximum(x, 0.0) + jnp.log1p(jnp.exp(-jnp.abs(x)))


def _split3(x):
    hi = x.astype(BF16).astype(F32)
    r = x - hi
    mid = r.astype(BF16).astype(F32)
    return hi, mid, r - mid


def _lane_parts(x):
    hi, mid, lo = _split3(x)
    lane = lax.broadcasted_iota(jnp.int32, x.shape, 1)
    return jnp.where(lane < HEADS, hi, jnp.where(lane < 2 * HEADS, mid, lo)).astype(BF16)


def _store_time_ordered(stage_ref, value, out_ref, cols):
    for j in range(value.shape[1] // LANES):
        stage_ref[j] = value[:, j * LANES:(j + 1) * LANES]
    out_ref[:, cols] = jnp.concatenate(
        [jnp.concatenate([stage_ref.at[j][pl.ds(s, ROW_TILES, stride=SUBLANES), :]
                          for s in range(SUBLANES)], axis=0)
         for j in range(value.shape[1] // LANES)], axis=1).astype(out_ref.dtype)


def _conv_halo(prev_tail, cur_tail):
    rows = cur_tail.shape[0]
    sub = lax.broadcasted_iota(jnp.int32, cur_tail.shape, 0) % SUBLANES
    return jnp.where(sub == 0,
                     pltpu.roll(prev_tail, rows - (SUBLANES - 1), axis=0),
                     pltpu.roll(cur_tail, 1, axis=0))


def _causal_conv(buf, raw, tail, w, taps, fresh):
    halo = SUBLANES * (taps - 1)
    cur_tail = raw[CHUNK - halo:CHUNK, :]
    buf[0:halo, :] = _conv_halo(jnp.where(fresh, 0.0, tail[...]), cur_tail)
    buf[halo:halo + CHUNK, :] = raw
    tail[...] = cur_tail
    acc = w[taps - 1:taps, :] * raw
    for k in range(taps - 1):
        acc = acc + w[k:k + 1, :] * buf[pl.ds(SUBLANES * k, CHUNK), :]
    return acc


def _mod_kernel(c_ref, w_ref, b_ref, o_ref):
    o_ref[...] = jnp.dot(c_ref[...], w_ref[...], precision=lax.Precision.HIGHEST,
                         preferred_element_type=F32) + b_ref[...]


def _modulation(c, w_mod, b_mod):
    bsz = c.shape[0]
    rows = -(-bsz // SUBLANES) * SUBLANES
    c_pad = jnp.zeros((rows, D_MODEL), F32).at[:bsz].set(c)
    out = pl.pallas_call(
        _mod_kernel,
        grid=(DEPTH,),
        in_specs=[
            pl.BlockSpec((rows, D_MODEL), lambda l: (0, 0)),
            pl.BlockSpec((None, D_MODEL, 3 * D_MODEL), lambda l: (l, 0, 0)),
            pl.BlockSpec((None, 1, 3 * D_MODEL), lambda l: (l, 0, 0)),
        ],
        out_specs=pl.BlockSpec((None, rows, 3 * D_MODEL), lambda l: (l, 0, 0)),
        out_shape=jax.ShapeDtypeStruct((DEPTH, rows, 3 * D_MODEL), F32),
        compiler_params=pltpu.CompilerParams(
            dimension_semantics=("arbitrary",), vmem_limit_bytes=VMEM_LIMIT_BYTES),
        name="adaln_modulation",
    )(c_pad, w_mod, b_mod.reshape(DEPTH, 1, 3 * D_MODEL))
    return out[:, :bsz]


def _mixer_kernel(n_batch, n_chunks, *refs):
    (x_cur, x_nxt,
     mod_ref, wx_ref, wbc_ref, wz_ref, wdt_ref, wsc_ref, convx_w_ref, convx_b_ref, convbc_w_ref,
     convbc_b_ref, dtb_ref, alog_ref, dskip_ref, normw_ref, convsc_ref, tri_ref, negmask_ref,
     esel_ref, eexp_ref, perm_ref,
     ynorm_ref, ybin_ref,
     u_ref, un_ref, tailx_ref, tailbc_ref, sctail_ref, tmp_ref, yout_ref, actx_ref, actbc_ref,
     xs_ref, xdtm_ref, xw_ref, m_ref, bt_ref, c_ref, hb_ref, hbn_ref, at_ref, atn_ref,
     state_ref) = refs
    L = CHUNK
    DT3, TE3, ACUM3, E3 = range(4)
    batch = pl.program_id(0)
    chunk = pl.program_id(1)
    seq_starts = chunk == 0
    last_chunk = chunk == n_chunks - 1
    nxt_batch = jnp.minimum(batch + last_chunk.astype(jnp.int32), n_batch - 1)
    lane = lax.broadcasted_iota(jnp.int32, (1, COL_BLOCK), 1)

    def norm_and_heads(x_ref, b):
        mod = mod_ref[b]
        shift = mod[:, 0:D_MODEL]
        scale = mod[:, D_MODEL:2 * D_MODEL]
        u = (_layer_norm(x_ref[...]) * (1.0 + scale) + shift).astype(BF16)
        un_ref[...] = _dot(perm_ref[...], u).astype(BF16)
        dt = _softplus(_dot(un_ref[...], wdt_ref[...]) + dtb_ref[...])
        a = dt * (-jnp.exp(alog_ref[...]))
        a_hi, a_mid, a_lo = _split3(a)
        tri = tri_ref[...]
        a_cum = (_dot(tri, a_hi.astype(BF16)) + _dot(tri, a_mid.astype(BF16))
                 + _dot(tri, a_lo.astype(BF16)))
        hbn_ref[DT3] = _lane_parts(dt)
        hbn_ref[TE3] = _lane_parts(jnp.exp(a_cum[L - 1:L, :] - a_cum))
        hbn_ref[E3] = _lane_parts(jnp.exp(a_cum))
        a_cum2 = a_cum * LOG2_E
        hbn_ref[ACUM3] = _lane_parts(a_cum2)
        atn_ref[...] = a_cum2.T

    def stage_a(p, slot, u, fresh):
        raw = _dot(u[...], wx_ref[p])
        actx_ref[slot] = _silu(_causal_conv(tmp_ref.at[2 * slot], raw, tailx_ref.at[p],
                                            convx_w_ref.at[p], SSM_CONV, fresh) + convx_b_ref[p])
        raw = _dot(u[...], wbc_ref[p])
        actbc_ref[slot] = _silu(_causal_conv(tmp_ref.at[2 * slot + 1], raw, tailbc_ref.at[p],
                                             convbc_w_ref.at[p], SSM_CONV, fresh) + convbc_b_ref[p])

    def stage_b(p, src, slot, hb, at):
        act = actx_ref[src]
        xs_ref[slot] = act
        xdt = act * _dot(hb[DT3], eexp_ref[p])
        for hh in range(HEADS_PER_GROUP):
            in_head = (lane >= hh * HEAD_DIM) & (lane < (hh + 1) * HEAD_DIM)
            xdtm_ref[slot, hh * L:(hh + 1) * L, :] = jnp.where(in_head, xdt, 0.0).astype(BF16)
        xw_ref[slot] = (xdt * _dot(hb[TE3], eexp_ref[p])).astype(BF16)
        act = actbc_ref[src]
        bmat = act[:, 0:STATE].astype(BF16)
        cmat = act[:, STATE:2 * STATE].astype(BF16)
        bt_ref[slot] = act[:, 0:STATE].T.astype(BF16)
        c_ref[slot] = cmat
        cb = lax.dot_general(cmat, bmat, (((1,), (1,)), ((), ())),
                             preferred_element_type=F32)
        acol4 = _dot(hb[ACUM3], esel_ref[p])
        for hh in range(HEADS_PER_GROUP):
            arow = at[pl.ds(p * HEADS_PER_GROUP + hh, 1), :]
            acol = acol4[:, hh * LANES:(hh + 1) * LANES]
            seg = jnp.concatenate([acol, acol], axis=1) - arow + negmask_ref[...]
            m_ref[slot, :, hh * L:(hh + 1) * L] = (cb * jnp.exp2(seg)).astype(BF16)

    def stage_c(g, slot):
        y = _dot(m_ref[slot], xdtm_ref[slot])
        state = jnp.where(seq_starts, 0.0, state_ref[g])
        escale = _dot(hb_ref[E3], eexp_ref[g])
        y = y + _dot(c_ref[slot], state.astype(BF16)) * escale
        state_ref[g] = state * escale[L - 1:L, :] + _dot(bt_ref[slot], xw_ref[slot])
        y = y + dskip_ref[g] * xs_ref[slot]
        y = y * _silu(_dot(u_ref[...], wz_ref[g]))
        y = y * lax.rsqrt(jnp.mean(y * y, axis=-1, keepdims=True) + RMS_EPS)
        _store_time_ordered(yout_ref.at[slot], y * normw_ref[g], ynorm_ref.at[g], slice(None))

    @pl.when((batch == 0) & (chunk == 0))
    def _():
        tailx_ref[...] = jnp.zeros_like(tailx_ref)
        tailbc_ref[...] = jnp.zeros_like(tailbc_ref)
        sctail_ref[...] = jnp.zeros_like(sctail_ref)
        state_ref[...] = jnp.zeros_like(state_ref)
        norm_and_heads(x_cur, batch)
        stage_a(0, 0, un_ref, True)
        stage_a(1, 1, un_ref, True)
        stage_b(0, 0, 0, hbn_ref, atn_ref)

    u_ref[...] = un_ref[...]
    hb_ref[...] = hbn_ref[...]
    at_ref[...] = atn_ref[...]

    for g in range(0, GROUPS - 2, 2):
        stage_c(g, 0)
        stage_b(g + 1, 1, 1, hb_ref, at_ref)
        stage_a(g + 2, 0, u_ref, seq_starts)
        stage_c(g + 1, 1)
        stage_b(g + 2, 0, 0, hb_ref, at_ref)
        stage_a(g + 3, 1, u_ref, seq_starts)
    stage_c(GROUPS - 2, 0)
    stage_b(GROUPS - 1, 1, 1, hb_ref, at_ref)
    norm_and_heads(x_nxt, nxt_batch)
    stage_a(0, 0, un_ref, last_chunk)
    stage_c(GROUPS - 1, 1)
    stage_b(0, 0, 0, hbn_ref, atn_ref)
    stage_a(1, 1, un_ref, last_chunk)

    for blk in range(SC_WIDTH // COL_BLOCK):
        cols = slice(blk * COL_BLOCK, (blk + 1) * COL_BLOCK)
        u = u_ref[...]
        sc_b = _dot(u, wsc_ref[:, blk * COL_BLOCK:(blk + 1) * COL_BLOCK])
        sc_c = _dot(u, wsc_ref[:, SC_WIDTH + blk * COL_BLOCK:SC_WIDTH + (blk + 1) * COL_BLOCK])
        sc_x = _dot(u, wsc_ref[:, 2 * SC_WIDTH + blk * COL_BLOCK:2 * SC_WIDTH + (blk + 1) * COL_BLOCK])
        sc_g = _dot(u, wsc_ref[:, 3 * SC_WIDTH + blk * COL_BLOCK:3 * SC_WIDTH + (blk + 1) * COL_BLOCK])
        v = _causal_conv(tmp_ref.at[4 + blk % 2], sc_c * sc_x, sctail_ref.at[blk], convsc_ref.at[blk],
                         SC_CONV, seq_starts)
        _store_time_ordered(yout_ref.at[2 + blk % 2], _silu(sc_g) * sc_b * v, ybin_ref, cols)


def _const(shape):
    return pl.BlockSpec(shape, lambda b, i: (0,) * len(shape), pipeline_mode=pl.Buffered(1))


def _mixer(x, mod, *consts):
    bsz, seq, _ = x.shape
    L = CHUNK
    assert seq % L == 0
    n_chunks = seq // L

    def next_block(b, i):
        wraps = (i == n_chunks - 1).astype(jnp.int32)
        return (jnp.minimum(b + wraps, bsz - 1), (i + 1) * (1 - wraps), 0)

    return pl.pallas_call(
        functools.partial(_mixer_kernel, bsz, n_chunks),
        grid=(bsz, n_chunks),
        in_specs=[
            pl.BlockSpec((None, L, D_MODEL), lambda b, i: (b, i, 0)),
            pl.BlockSpec((None, L, D_MODEL), next_block),
        ] + [_const(a.shape) for a in (mod,) + consts],
        out_specs=[
            pl.BlockSpec((None, GROUPS, L, GROUP_WIDTH), lambda b, i: (b, 0, i, 0)),
            pl.BlockSpec((None, L, SC_WIDTH), lambda b, i: (b, i, 0)),
        ],
        out_shape=[
            jax.ShapeDtypeStruct((bsz, GROUPS, seq, GROUP_WIDTH), BF16),
            jax.ShapeDtypeStruct((bsz, seq, SC_WIDTH), BF16),
        ],
        scratch_shapes=[
            pltpu.VMEM((L, D_MODEL), BF16),
            pltpu.VMEM((L, D_MODEL), BF16),
            pltpu.VMEM((GROUPS, SSM_HALO, GROUP_WIDTH), F32),
            pltpu.VMEM((GROUPS, SSM_HALO, 2 * STATE), F32),
            pltpu.VMEM((SC_WIDTH // COL_BLOCK, SC_HALO, COL_BLOCK), F32),
            pltpu.VMEM((6, SSM_HALO + L, COL_BLOCK), F32),
            pltpu.VMEM((4, COL_BLOCK // LANES, L, LANES), F32),
            pltpu.VMEM((2, L, GROUP_WIDTH), F32),
            pltpu.VMEM((2, L, 2 * STATE), F32),
            pltpu.VMEM((2, L, GROUP_WIDTH), F32),
            pltpu.VMEM((2, HEADS_PER_GROUP * L, GROUP_WIDTH), BF16),
            pltpu.VMEM((2, L, GROUP_WIDTH), BF16),
            pltpu.VMEM((2, L, HEADS_PER_GROUP * L), BF16),
            pltpu.VMEM((2, STATE, L), BF16),
            pltpu.VMEM((2, L, STATE), BF16),
            pltpu.VMEM((4, L, LANES), BF16),
            pltpu.VMEM((4, L, LANES), BF16),
            pltpu.VMEM((LANES, L), F32),
            pltpu.VMEM((LANES, L), F32),
            pltpu.VMEM((GROUPS, STATE, GROUP_WIDTH), F32),
        ],
        compiler_params=pltpu.CompilerParams(
            dimension_semantics=("arbitrary", "arbitrary"), vmem_limit_bytes=VMEM_LIMIT_BYTES),
        name="ssd_shortconv_mixer",
    )(x, x, mod, *consts)


def _merge_kernel(x_ref, mod_ref, ynorm_ref, ybin_ref, wgate_ref, bgate_ref, wa_ref, wb_ref,
                  wout_ref, lng_ref, lnb_ref, o_ref):
    x = x_ref[...]
    shift = mod_ref[:, 0:D_MODEL]
    scale = mod_ref[:, D_MODEL:2 * D_MODEL]
    gate = mod_ref[:, 2 * D_MODEL:3 * D_MODEL]
    u = (_layer_norm(x) * (1.0 + scale) + shift).astype(BF16)
    gates = jax.nn.sigmoid(_dot(u, wgate_ref[...]) + bgate_ref[...])
    y_a = _dot(ynorm_ref[0], wa_ref[0])
    for g in range(1, GROUPS):
        y_a = y_a + _dot(ynorm_ref[g], wa_ref[g])
    y_b = _dot(ybin_ref[...], wb_ref[...])
    merged = gates[:, 0:D_MODEL] * y_a + gates[:, D_MODEL:2 * D_MODEL] * y_b
    mixed = _dot(merged.astype(BF16), wout_ref[...])
    o_ref[...] = _layer_norm(DN_ALPHA * x + gate * mixed) * lng_ref[...] + lnb_ref[...]


def _merge(x, mod, ynorm, ybin, wgate, bgate, wa, wb, wout, lng, lnb):
    bsz, seq, _ = x.shape
    tm = min(MERGE_TILE, seq)
    assert seq % tm == 0
    consts = (wgate, bgate, wa, wb, wout, lng, lnb)
    return pl.pallas_call(
        _merge_kernel,
        grid=(bsz, seq // tm),
        in_specs=[
            pl.BlockSpec((None, tm, D_MODEL), lambda b, i: (b, i, 0)),
            pl.BlockSpec((None, 1, 3 * D_MODEL), lambda b, i: (b, 0, 0)),
            pl.BlockSpec((None, GROUPS, tm, GROUP_WIDTH), lambda b, i: (b, 0, i, 0)),
            pl.BlockSpec((None, tm, SC_WIDTH), lambda b, i: (b, i, 0)),
        ] + [_const(a.shape) for a in consts],
        out_specs=pl.BlockSpec((None, tm, D_MODEL), lambda b, i: (b, i, 0)),
        out_shape=jax.ShapeDtypeStruct((bsz, seq, D_MODEL), F32),
        compiler_params=pltpu.CompilerParams(
            dimension_semantics=("arbitrary", "arbitrary"), vmem_limit_bytes=VMEM_LIMIT_BYTES),
        name="gated_merge",
    )(x, mod, ynorm, ybin, *consts)


def _head_lanes(v):
    pad = jnp.zeros((LANES - HEAD_REPLICAS * HEADS,), F32)
    return jnp.concatenate([v] * HEAD_REPLICAS + [pad]).reshape(1, LANES)


def _selection_constants():
    L = CHUNK
    idx = jnp.arange(L)
    time = ROW_TILES * (idx % SUBLANES) + idx // SUBLANES
    causal = time[:, None] >= time[None, :]
    tri = causal.astype(BF16)
    negmask = jnp.where(causal, 0.0, -jnp.inf).astype(F32)
    j = jnp.arange(LANES)
    head_of_lane = jnp.where(j < HEAD_REPLICAS * HEADS, j % HEADS, -1)
    ch_head = jnp.arange(D_INNER) // HEAD_DIM
    eexp = (head_of_lane[:, None] == ch_head[None, :]).astype(BF16)
    eexp = eexp.reshape(LANES, GROUPS, GROUP_WIDTH).transpose(1, 0, 2)
    sel_head = jnp.arange(HEADS * LANES) // LANES
    esel = (head_of_lane[:, None] == sel_head[None, :]).astype(BF16)
    esel = esel.reshape(LANES, GROUPS, HEADS_PER_GROUP * LANES).transpose(1, 0, 2)
    perm = (time[:, None] == idx[None, :]).astype(BF16)
    return tri, negmask, esel, eexp, perm


def _by_group(a, width):
    return a.reshape(a.shape[0], GROUPS, width).transpose(1, 0, 2)


def _bc_by_group(a):
    b = a[:, D_INNER:D_INNER + GROUPS * STATE].reshape(a.shape[0], GROUPS, STATE)
    c = a[:, D_INNER + GROUPS * STATE:].reshape(a.shape[0], GROUPS, STATE)
    return jnp.concatenate([b, c], axis=2).transpose(1, 0, 2)


def kernel(x, c, w_mod, b_mod, w_in, conv_ssm_w, conv_ssm_b, dt_bias, a_log, d_skip, ssm_norm_w,
           conv_sc_w, w_branch_a, w_branch_b, w_gate, b_gate, w_out, ln_g, ln_b):
    bsz = x.shape[0]
    mod_all = _modulation(c, w_mod, b_mod)
    tri, negmask, esel, eexp, perm = _selection_constants()
    for l in range(DEPTH):
        mod = mod_all[l].reshape(bsz, 1, 3 * D_MODEL)
        w = w_in[l]
        wxbc = w[:, SPLIT_Z:SPLIT_XBC].astype(BF16)
        wdt_h = w[:, SPLIT_XBC:SPLIT_DT].astype(BF16)
        wdt = jnp.concatenate(
            [wdt_h] * HEAD_REPLICAS + [jnp.zeros((D_MODEL, LANES - HEAD_REPLICAS * HEADS), BF16)], axis=1)
        convw, convb = conv_ssm_w[l], conv_ssm_b[l].reshape(1, XBC)
        ynorm, ybin = _mixer(
            x, mod,
            _by_group(wxbc[:, :D_INNER], GROUP_WIDTH), _bc_by_group(wxbc),
            _by_group(w[:, :SPLIT_Z].astype(BF16), GROUP_WIDTH), wdt, w[:, SPLIT_DT:].astype(BF16),
            _by_group(convw[:, :D_INNER], GROUP_WIDTH), _by_group(convb[:, :D_INNER], GROUP_WIDTH),
            _bc_by_group(convw), _bc_by_group(convb),
            _head_lanes(dt_bias[l]), _head_lanes(a_log[l]),
            jnp.repeat(d_skip[l], HEAD_DIM).reshape(GROUPS, 1, GROUP_WIDTH),
            ssm_norm_w[l].reshape(GROUPS, 1, GROUP_WIDTH),
            conv_sc_w[l].reshape(SC_CONV, SC_WIDTH // COL_BLOCK, COL_BLOCK).transpose(1, 0, 2),
            tri, negmask, esel, eexp, perm)
        x = _merge(
            x, mod, ynorm, ybin,
            w_gate[l].astype(BF16), b_gate[l].reshape(1, 2 * D_MODEL),
            w_branch_a[l].astype(BF16).reshape(GROUPS, GROUP_WIDTH, D_MODEL),
            w_branch_b[l].astype(BF16), w_out[l].astype(BF16),
            ln_g[l].reshape(1, D_MODEL), ln_b[l].reshape(1, D_MODEL))
    return x
```

```python
import functools

import jax
import jax.numpy as jnp
from jax import lax
from jax.experimental import pallas as pl
from jax.experimental.pallas import tpu as pltpu

D_MODEL = 1024
DEPTH = 2
D_INNER = 2048
HEAD_DIM = 64
HEADS = 32
GROUPS = 8
STATE = 128
SSM_CONV = 4
CHUNK = 256
XBC = D_INNER + 2 * GROUPS * STATE
SC_WIDTH = 1024
SC_CONV = 3
HEADS_PER_GROUP = HEADS // GROUPS
GROUP_WIDTH = HEADS_PER_GROUP * HEAD_DIM
SPLIT_Z = D_INNER
SPLIT_XBC = SPLIT_Z + XBC
SPLIT_DT = SPLIT_XBC + HEADS
DN_ALPHA = (2 * DEPTH) ** 0.25
LN_EPS = 1e-5
RMS_EPS = 1e-5
LOG2_E = 1.4426950408889634

LANES = 128
SUBLANES = 8
ROW_TILES = CHUNK // SUBLANES
SSM_HALO = SUBLANES * (SSM_CONV - 1)
SC_HALO = SUBLANES * (SC_CONV - 1)
COL_BLOCK = 256
HEAD_REPLICAS = 3
MERGE_TILE = 512
VMEM_LIMIT_BYTES = 56 * 1024 * 1024

F32 = jnp.float32
BF16 = jnp.bfloat16


def _dot(a, b):
    return jnp.dot(a, b, preferred_element_type=F32)


def _layer_norm(x):
    mu = jnp.mean(x, axis=-1, keepdims=True)
    xc = x - mu
    var = jnp.mean(xc * xc, axis=-1, keepdims=True)
    return xc * lax.rsqrt(var + LN_EPS)


def _silu(x):
    h = 0.5 * x
    return h + h * jnp.tanh(h)


def _softplus(x):
    return jnp.maximum(x, 0.0) + jnp.log1p(jnp.exp(-jnp.abs(x)))


def _split3(x):
    hi = x.astype(BF16).astype(F32)
    r = x - hi
    mid = r.astype(BF16).astype(F32)
    return hi, mid, r - mid


def _lane_parts(x):
    hi, mid, lo = _split3(x)
    lane = lax.broadcasted_iota(jnp.int32, x.shape, 1)
    return jnp.where(lane < HEADS, hi, jnp.where(lane < 2 * HEADS, mid, lo)).astype(BF16)


def _store_time_ordered(stage_ref, value, out_ref, cols):
    for j in range(value.shape[1] // LANES):
        stage_ref[j] = value[:, j * LANES:(j + 1) * LANES]
    out_ref[:, cols] = jnp.concatenate(
        [jnp.concatenate([stage_ref.at[j][pl.ds(s, ROW_TILES, stride=SUBLANES), :]
                          for s in range(SUBLANES)], axis=0)
         for j in range(value.shape[1] // LANES)], axis=1).astype(out_ref.dtype)


def _conv_halo(prev_tail, cur_tail):
    rows = cur_tail.shape[0]
    sub = lax.broadcasted_iota(jnp.int32, cur_tail.shape, 0) % SUBLANES
    return jnp.where(sub == 0,
                     pltpu.roll(prev_tail, rows - (SUBLANES - 1), axis=0),
                     pltpu.roll(cur_tail, 1, axis=0))


def _causal_conv(buf, raw, tail, w, taps, fresh):
    halo = SUBLANES * (taps - 1)
    cur_tail = raw[CHUNK - halo:CHUNK, :]
    buf[0:halo, :] = _conv_halo(jnp.where(fresh, 0.0, tail[...]), cur_tail)
    buf[halo:halo + CHUNK, :] = raw
    tail[...] = cur_tail
    acc = w[taps - 1:taps, :] * raw
    for k in range(taps - 1):
        acc = acc + w[k:k + 1, :] * buf[pl.ds(SUBLANES * k, CHUNK), :]
    return acc


def _mod_kernel(c_ref, w_ref, b_ref, o_ref):
    o_ref[...] = jnp.dot(c_ref[...], w_ref[...], precision=lax.Precision.HIGHEST,
                         preferred_element_type=F32) + b_ref[...]


def _modulation(c, w_mod, b_mod):
    bsz = c.shape[0]
    rows = -(-bsz // SUBLANES) * SUBLANES
    c_pad = jnp.zeros((rows, D_MODEL), F32).at[:bsz].set(c)
    out = pl.pallas_call(
        _mod_kernel,
        grid=(DEPTH,),
        in_specs=[
            pl.BlockSpec((rows, D_MODEL), lambda l: (0, 0)),
            pl.BlockSpec((None, D_MODEL, 3 * D_MODEL), lambda l: (l, 0, 0)),
            pl.BlockSpec((None, 1, 3 * D_MODEL), lambda l: (l, 0, 0)),
        ],
        out_specs=pl.BlockSpec((None, rows, 3 * D_MODEL), lambda l: (l, 0, 0)),
        out_shape=jax.ShapeDtypeStruct((DEPTH, rows, 3 * D_MODEL), F32),
        compiler_params=pltpu.CompilerParams(
            dimension_semantics=("arbitrary",), vmem_limit_bytes=VMEM_LIMIT_BYTES),
        name="adaln_modulation",
    )(c_pad, w_mod, b_mod.reshape(DEPTH, 1, 3 * D_MODEL))
    return out[:, :bsz]


def _mixer_kernel(n_batch, n_chunks, *refs):
    (x_cur, x_nxt,
     mod_ref, wx_ref, wbc_ref, wz_ref, wdt_ref, wsc_ref, convx_w_ref, convx_b_ref, convbc_w_ref,
     convbc_b_ref, dtb_ref, alog_ref, dskip_ref, normw_ref, convsc_ref, tri_ref, negmask_ref,
     eexp_ref, perm_ref,
     ynorm_ref, ybin_ref,
     u_ref, un_ref, tailx_ref, tailbc_ref, sctail_ref, tmp_ref, yout_ref, actx_ref, actbc_ref,
     xs_ref, xdtm_ref, xw_ref, m_ref, bt_ref, c_ref, hb_ref, hbn_ref, at_ref, atn_ref,
     ac_ref, acn_ref, state_ref) = refs
    L = CHUNK
    DT3, TE3, E3 = range(3)
    batch = pl.program_id(0)
    chunk = pl.program_id(1)
    seq_starts = chunk == 0
    last_chunk = chunk == n_chunks - 1
    nxt_batch = jnp.minimum(batch + last_chunk.astype(jnp.int32), n_batch - 1)
    lane = lax.broadcasted_iota(jnp.int32, (1, COL_BLOCK), 1)

    def norm_and_heads(x_ref, b):
        mod = mod_ref[b]
        shift = mod[:, 0:D_MODEL]
        scale = mod[:, D_MODEL:2 * D_MODEL]
        u = (_layer_norm(x_ref[...]) * (1.0 + scale) + shift).astype(BF16)
        un_ref[...] = _dot(perm_ref[...], u).astype(BF16)
        dt = _softplus(_dot(un_ref[...], wdt_ref[...]) + dtb_ref[...])
        a = dt * (-jnp.exp(alog_ref[...]))
        a_hi, a_mid, a_lo = _split3(a)
        tri = tri_ref[...]
        a_cum = (_dot(tri, a_hi.astype(BF16)) + _dot(tri, a_mid.astype(BF16))
                 + _dot(tri, a_lo.astype(BF16)))
        hbn_ref[DT3] = _lane_parts(dt)
        hbn_ref[TE3] = _lane_parts(jnp.exp(a_cum[L - 1:L, :] - a_cum))
        hbn_ref[E3] = _lane_parts(jnp.exp(a_cum))
        a_cum2 = a_cum * LOG2_E
        acn_ref[...] = a_cum2
        atn_ref[...] = a_cum2.T

    def stage_a(p, slot, u, fresh):
        raw = _dot(u[...], wx_ref[p])
        actx_ref[slot] = _silu(_causal_conv(tmp_ref.at[2 * slot], raw, tailx_ref.at[p],
                                            convx_w_ref.at[p], SSM_CONV, fresh) + convx_b_ref[p])
        raw = _dot(u[...], wbc_ref[p])
        actbc_ref[slot] = _silu(_causal_conv(tmp_ref.at[2 * slot + 1], raw, tailbc_ref.at[p],
                                             convbc_w_ref.at[p], SSM_CONV, fresh) + convbc_b_ref[p])

    def stage_b(p, src, slot, hb, at, ac):
        act = actx_ref[src]
        xs_ref[slot] = act
        xdt = act * _dot(hb[DT3], eexp_ref[p])
        for hh in range(HEADS_PER_GROUP):
            in_head = (lane >= hh * HEAD_DIM) & (lane < (hh + 1) * HEAD_DIM)
            xdtm_ref[slot, hh * L:(hh + 1) * L, :] = jnp.where(in_head, xdt, 0.0).astype(BF16)
        xw_ref[slot] = (xdt * _dot(hb[TE3], eexp_ref[p])).astype(BF16)
        act = actbc_ref[src]
        bmat = act[:, 0:STATE].astype(BF16)
        cmat = act[:, STATE:2 * STATE].astype(BF16)
        bt_ref[slot] = act[:, 0:STATE].T.astype(BF16)
        c_ref[slot] = cmat
        cb = lax.dot_general(cmat, bmat, (((1,), (1,)), ((), ())),
                             preferred_element_type=F32)
        for hh in range(HEADS_PER_GROUP):
            h = p * HEADS_PER_GROUP + hh
            arow = at[pl.ds(h, 1), :]
            acol = ac[:, h:h + 1]
            seg = acol - arow + negmask_ref[...]
            m_ref[slot, :, hh * L:(hh + 1) * L] = (cb * jnp.exp2(seg)).astype(BF16)

    def stage_c(g, slot):
        y = _dot(m_ref[slot], xdtm_ref[slot])
        state = jnp.where(seq_starts, 0.0, state_ref[g])
        escale = _dot(hb_ref[E3], eexp_ref[g])
        y = y + _dot(c_ref[slot], state.astype(BF16)) * escale
        state_ref[g] = state * escale[L - 1:L, :] + _dot(bt_ref[slot], xw_ref[slot])
        y = y + dskip_ref[g] * xs_ref[slot]
        y = y * _silu(_dot(u_ref[...], wz_ref[g]))
        y = y * lax.rsqrt(jnp.mean(y * y, axis=-1, keepdims=True) + RMS_EPS)
        _store_time_ordered(yout_ref.at[slot], y * normw_ref[g], ynorm_ref.at[g], slice(None))

    @pl.when((batch == 0) & (chunk == 0))
    def _():
        tailx_ref[...] = jnp.zeros_like(tailx_ref)
        tailbc_ref[...] = jnp.zeros_like(tailbc_ref)
        sctail_ref[...] = jnp.zeros_like(sctail_ref)
        state_ref[...] = jnp.zeros_like(state_ref)
        norm_and_heads(x_cur, batch)
        stage_a(0, 0, un_ref, True)
        stage_a(1, 1, un_ref, True)
        stage_b(0, 0, 0, hbn_ref, atn_ref, acn_ref)

    u_ref[...] = un_ref[...]
    hb_ref[...] = hbn_ref[...]
    at_ref[...] = atn_ref[...]
    ac_ref[...] = acn_ref[...]

    for g in range(0, GROUPS - 2, 2):
        stage_c(g, 0)
        stage_b(g + 1, 1, 1, hb_ref, at_ref, ac_ref)
        stage_a(g + 2, 0, u_ref, seq_starts)
        stage_c(g + 1, 1)
        stage_b(g + 2, 0, 0, hb_ref, at_ref, ac_ref)
        stage_a(g + 3, 1, u_ref, seq_starts)
    stage_c(GROUPS - 2, 0)
    stage_b(GROUPS - 1, 1, 1, hb_ref, at_ref, ac_ref)
    norm_and_heads(x_nxt, nxt_batch)
    stage_a(0, 0, un_ref, last_chunk)
    stage_c(GROUPS - 1, 1)
    stage_b(0, 0, 0, hbn_ref, atn_ref, acn_ref)
    stage_a(1, 1, un_ref, last_chunk)

    for blk in range(SC_WIDTH // COL_BLOCK):
        cols = slice(blk * COL_BLOCK, (blk + 1) * COL_BLOCK)
        u = u_ref[...]
        sc_b = _dot(u, wsc_ref[:, blk * COL_BLOCK:(blk + 1) * COL_BLOCK])
        sc_c = _dot(u, wsc_ref[:, SC_WIDTH + blk * COL_BLOCK:SC_WIDTH + (blk + 1) * COL_BLOCK])
        sc_x = _dot(u, wsc_ref[:, 2 * SC_WIDTH + blk * COL_BLOCK:2 * SC_WIDTH + (blk + 1) * COL_BLOCK])
        sc_g = _dot(u, wsc_ref[:, 3 * SC_WIDTH + blk * COL_BLOCK:3 * SC_WIDTH + (blk + 1) * COL_BLOCK])
        v = _causal_conv(tmp_ref.at[4 + blk % 2], sc_c * sc_x, sctail_ref.at[blk], convsc_ref.at[blk],
                         SC_CONV, seq_starts)
        _store_time_ordered(yout_ref.at[2 + blk % 2], _silu(sc_g) * sc_b * v, ybin_ref, cols)


def _const(shape):
    return pl.BlockSpec(shape, lambda b, i: (0,) * len(shape), pipeline_mode=pl.Buffered(1))


def _mixer(x, mod, *consts):
    bsz, seq, _ = x.shape
    L = CHUNK
    assert seq % L == 0
    n_chunks = seq // L

    def next_block(b, i):
        wraps = (i == n_chunks - 1).astype(jnp.int32)
        return (jnp.minimum(b + wraps, bsz - 1), (i + 1) * (1 - wraps), 0)

    return pl.pallas_call(
        functools.partial(_mixer_kernel, bsz, n_chunks),
        grid=(bsz, n_chunks),
        in_specs=[
            pl.BlockSpec((None, L, D_MODEL), lambda b, i: (b, i, 0)),
            pl.BlockSpec((None, L, D_MODEL), next_block),
        ] + [_const(a.shape) for a in (mod,) + consts],
        out_specs=[
            pl.BlockSpec((None, GROUPS, L, GROUP_WIDTH), lambda b, i: (b, 0, i, 0)),
            pl.BlockSpec((None, L, SC_WIDTH), lambda b, i: (b, i, 0)),
        ],
        out_shape=[
            jax.ShapeDtypeStruct((bsz, GROUPS, seq, GROUP_WIDTH), BF16),
            jax.ShapeDtypeStruct((bsz, seq, SC_WIDTH), BF16),
        ],
        scratch_shapes=[
            pltpu.VMEM((L, D_MODEL), BF16),
            pltpu.VMEM((L, D_MODEL), BF16),
            pltpu.VMEM((GROUPS, SSM_HALO, GROUP_WIDTH), F32),
            pltpu.VMEM((GROUPS, SSM_HALO, 2 * STATE), F32),
            pltpu.VMEM((SC_WIDTH // COL_BLOCK, SC_HALO, COL_BLOCK), F32),
            pltpu.VMEM((6, SSM_HALO + L, COL_BLOCK), F32),
            pltpu.VMEM((4, COL_BLOCK // LANES, L, LANES), F32),
            pltpu.VMEM((2, L, GROUP_WIDTH), F32),
            pltpu.VMEM((2, L, 2 * STATE), F32),
            pltpu.VMEM((2, L, GROUP_WIDTH), F32),
            pltpu.VMEM((2, HEADS_PER_GROUP * L, GROUP_WIDTH), BF16),
            pltpu.VMEM((2, L, GROUP_WIDTH), BF16),
            pltpu.VMEM((2, L, HEADS_PER_GROUP * L), BF16),
            pltpu.VMEM((2, STATE, L), BF16),
            pltpu.VMEM((2, L, STATE), BF16),
            pltpu.VMEM((3, L, LANES), BF16),
            pltpu.VMEM((3, L, LANES), BF16),
            pltpu.VMEM((LANES, L), F32),
            pltpu.VMEM((LANES, L), F32),
            pltpu.VMEM((L, LANES), F32),
            pltpu.VMEM((L, LANES), F32),
            pltpu.VMEM((GROUPS, STATE, GROUP_WIDTH), F32),
        ],
        compiler_params=pltpu.CompilerParams(
            dimension_semantics=("arbitrary", "arbitrary"), vmem_limit_bytes=VMEM_LIMIT_BYTES),
        name="ssd_shortconv_mixer",
    )(x, x, mod, *consts)


def _merge_kernel(x_ref, mod_ref, ynorm_ref, ybin_ref, wgate_ref, bgate_ref, wa_ref, wb_ref,
                  wout_ref, lng_ref, lnb_ref, o_ref):
    x = x_ref[...]
    shift = mod_ref[:, 0:D_MODEL]
    scale = mod_ref[:, D_MODEL:2 * D_MODEL]
    gate = mod_ref[:, 2 * D_MODEL:3 * D_MODEL]
    u = (_layer_norm(x) * (1.0 + scale) + shift).astype(BF16)
    gates = jax.nn.sigmoid(_dot(u, wgate_ref[...]) + bgate_ref[...])
    y_a = _dot(ynorm_ref[0], wa_ref[0])
    for g in range(1, GROUPS):
        y_a = y_a + _dot(ynorm_ref[g], wa_ref[g])
    y_b = _dot(ybin_ref[...], wb_ref[...])
    merged = gates[:, 0:D_MODEL] * y_a + gates[:, D_MODEL:2 * D_MODEL] * y_b
    mixed = _dot(merged.astype(BF16), wout_ref[...])
    o_ref[...] = _layer_norm(DN_ALPHA * x + gate * mixed) * lng_ref[...] + lnb_ref[...]


def _merge(x, mod, ynorm, ybin, wgate, bgate, wa, wb, wout, lng, lnb):
    bsz, seq, _ = x.shape
    tm = min(MERGE_TILE, seq)
    assert seq % tm == 0
    consts = (wgate, bgate, wa, wb, wout, lng, lnb)
    return pl.pallas_call(
        _merge_kernel,
        grid=(bsz, seq // tm),
        in_specs=[
            pl.BlockSpec((None, tm, D_MODEL), lambda b, i: (b, i, 0)),
            pl.BlockSpec((None, 1, 3 * D_MODEL), lambda b, i: (b, 0, 0)),
            pl.BlockSpec((None, GROUPS, tm, GROUP_WIDTH), lambda b, i: (b, 0, i, 0)),
            pl.BlockSpec((None, tm, SC_WIDTH), lambda b, i: (b, i, 0)),
        ] + [_const(a.shape) for a in consts],
        out_specs=pl.BlockSpec((None, tm, D_MODEL), lambda b, i: (b, i, 0)),
        out_shape=jax.ShapeDtypeStruct((bsz, seq, D_MODEL), F32),
        compiler_params=pltpu.CompilerParams(
            dimension_semantics=("arbitrary", "arbitrary"), vmem_limit_bytes=VMEM_LIMIT_BYTES),
        name="gated_merge",
    )(x, mod, ynorm, ybin, *consts)


def _head_lanes(v):
    pad = jnp.zeros((LANES - HEAD_REPLICAS * HEADS,), F32)
    return jnp.concatenate([v] * HEAD_REPLICAS + [pad]).reshape(1, LANES)


def _selection_constants():
    L = CHUNK
    idx = jnp.arange(L)
    time = ROW_TILES * (idx % SUBLANES) + idx // SUBLANES
    causal = time[:, None] >= time[None, :]
    tri = causal.astype(BF16)
    negmask = jnp.where(causal, 0.0, -jnp.inf).astype(F32)
    j = jnp.arange(LANES)
    head_of_lane = jnp.where(j < HEAD_REPLICAS * HEADS, j % HEADS, -1)
    ch_head = jnp.arange(D_INNER) // HEAD_DIM
    eexp = (head_of_lane[:, None] == ch_head[None, :]).astype(BF16)
    eexp = eexp.reshape(LANES, GROUPS, GROUP_WIDTH).transpose(1, 0, 2)
    perm = (time[:, None] == idx[None, :]).astype(BF16)
    return tri, negmask, eexp, perm


def _by_group(a, width):
    return a.reshape(a.shape[0], GROUPS, width).transpose(1, 0, 2)


def _bc_by_group(a):
    b = a[:, D_INNER:D_INNER + GROUPS * STATE].reshape(a.shape[0], GROUPS, STATE)
    c = a[:, D_INNER + GROUPS * STATE:].reshape(a.shape[0], GROUPS, STATE)
    return jnp.concatenate([b, c], axis=2).transpose(1, 0, 2)


def kernel(x, c, w_mod, b_mod, w_in, conv_ssm_w, conv_ssm_b, dt_bias, a_log, d_skip, ssm_norm_w,
           conv_sc_w, w_branch_a, w_branch_b, w_gate, b_gate, w_out, ln_g, ln_b):
    bsz = x.shape[0]
    mod_all = _modulation(c, w_mod, b_mod)
    tri, negmask, eexp, perm = _selection_constants()
    for l in range(DEPTH):
        mod = mod_all[l].reshape(bsz, 1, 3 * D_MODEL)
        w = w_in[l]
        wxbc = w[:, SPLIT_Z:SPLIT_XBC].astype(BF16)
        wdt_h = w[:, SPLIT_XBC:SPLIT_DT].astype(BF16)
        wdt = jnp.concatenate(
            [wdt_h] * HEAD_REPLICAS + [jnp.zeros((D_MODEL, LANES - HEAD_REPLICAS * HEADS), BF16)], axis=1)
        convw, convb = conv_ssm_w[l], conv_ssm_b[l].reshape(1, XBC)
        ynorm, ybin = _mixer(
            x, mod,
            _by_group(wxbc[:, :D_INNER], GROUP_WIDTH), _bc_by_group(wxbc),
            _by_group(w[:, :SPLIT_Z].astype(BF16), GROUP_WIDTH), wdt, w[:, SPLIT_DT:].astype(BF16),
            _by_group(convw[:, :D_INNER], GROUP_WIDTH), _by_group(convb[:, :D_INNER], GROUP_WIDTH),
            _bc_by_group(convw), _bc_by_group(convb),
            _head_lanes(dt_bias[l]), _head_lanes(a_log[l]),
            jnp.repeat(d_skip[l], HEAD_DIM).reshape(GROUPS, 1, GROUP_WIDTH),
            ssm_norm_w[l].reshape(GROUPS, 1, GROUP_WIDTH),
            conv_sc_w[l].reshape(SC_CONV, SC_WIDTH // COL_BLOCK, COL_BLOCK).transpose(1, 0, 2),
            tri, negmask, eexp, perm)
        x = _merge(
            x, mod, ynorm, ybin,
            w_gate[l].astype(BF16), b_gate[l].reshape(1, 2 * D_MODEL),
            w_branch_a[l].astype(BF16).reshape(GROUPS, GROUP_WIDTH, D_MODEL),
            w_branch_b[l].astype(BF16), w_out[l].astype(BF16),
            ln_g[l].reshape(1, D_MODEL), ln_b[l].reshape(1, D_MODEL))
    return x
```

```python
import functools

import jax
import jax.numpy as jnp
from jax import lax
from jax.experimental import pallas as pl
from jax.experimental.pallas import tpu as pltpu

D_MODEL = 1024
DEPTH = 2
D_INNER = 2048
HEAD_DIM = 64
HEADS = 32
GROUPS = 8
STATE = 128
SSM_CONV = 4
CHUNK = 256
XBC = D_INNER + 2 * GROUPS * STATE
SC_WIDTH = 1024
SC_CONV = 3
HEADS_PER_GROUP = HEADS // GROUPS
GROUP_WIDTH = HEADS_PER_GROUP * HEAD_DIM
SPLIT_Z = D_INNER
SPLIT_XBC = SPLIT_Z + XBC
SPLIT_DT = SPLIT_XBC + HEADS
DN_ALPHA = (2 * DEPTH) ** 0.25
LN_EPS = 1e-5
RMS_EPS = 1e-5
LOG2_E = 1.4426950408889634

LANES = 128
SUBLANES = 8
ROW_TILES = CHUNK // SUBLANES
SSM_HALO = SUBLANES * (SSM_CONV - 1)
SC_HALO = SUBLANES * (SC_CONV - 1)
COL_BLOCK = 256
HEAD_REPLICAS = 3
GP_CONVX, GP_CONVX_B, GP_CONVBC, GP_CONVBC_B, GP_D_SKIP, GP_NORM_W = 0, 4, 5, 9, 10, 11
HP_DT_BIAS, HP_A_LOG = 0, 1
MERGE_TILE = 512
VMEM_LIMIT_BYTES = 56 * 1024 * 1024

F32 = jnp.float32
BF16 = jnp.bfloat16


def _dot(a, b):
    return jnp.dot(a, b, preferred_element_type=F32)


def _layer_norm(x):
    mu = jnp.mean(x, axis=-1, keepdims=True)
    xc = x - mu
    var = jnp.mean(xc * xc, axis=-1, keepdims=True)
    return xc * lax.rsqrt(var + LN_EPS)


def _silu(x):
    h = 0.5 * x
    return h + h * jnp.tanh(h)


def _softplus(x):
    return jnp.maximum(x, 0.0) + jnp.log1p(jnp.exp(-jnp.abs(x)))


def _split3(x):
    hi = x.astype(BF16).astype(F32)
    r = x - hi
    mid = r.astype(BF16).astype(F32)
    return hi, mid, r - mid


def _lane_parts(x):
    hi, mid, lo = _split3(x)
    lane = lax.broadcasted_iota(jnp.int32, x.shape, 1)
    return jnp.where(lane < HEADS, hi, jnp.where(lane < 2 * HEADS, mid, lo)).astype(BF16)


def _store_time_ordered(stage_ref, value, out_ref, cols):
    for j in range(value.shape[1] // LANES):
        stage_ref[j] = value[:, j * LANES:(j + 1) * LANES]
    out_ref[:, cols] = jnp.concatenate(
        [jnp.concatenate([stage_ref.at[j][pl.ds(s, ROW_TILES, stride=SUBLANES), :]
                          for s in range(SUBLANES)], axis=0)
         for j in range(value.shape[1] // LANES)], axis=1).astype(out_ref.dtype)


def _conv_halo(prev_tail, cur_tail):
    rows = cur_tail.shape[0]
    sub = lax.broadcasted_iota(jnp.int32, cur_tail.shape, 0) % SUBLANES
    return jnp.where(sub == 0,
                     pltpu.roll(prev_tail, rows - (SUBLANES - 1), axis=0),
                     pltpu.roll(cur_tail, 1, axis=0))


def _causal_conv(buf, raw, tail, w, row0, taps, fresh):
    halo = SUBLANES * (taps - 1)
    cur_tail = raw[CHUNK - halo:CHUNK, :]
    buf[0:halo, :] = _conv_halo(jnp.where(fresh, 0.0, tail[...]), cur_tail)
    buf[halo:halo + CHUNK, :] = raw
    tail[...] = cur_tail
    acc = w[row0 + taps - 1:row0 + taps, :] * raw
    for k in range(taps - 1):
        acc = acc + w[row0 + k:row0 + k + 1, :] * buf[pl.ds(SUBLANES * k, CHUNK), :]
    return acc


def _mod_kernel(c_ref, w_ref, b_ref, o_ref):
    o_ref[...] = jnp.dot(c_ref[...], w_ref[...], precision=lax.Precision.HIGHEST,
                         preferred_element_type=F32) + b_ref[...]


def _modulation(c, w_mod, b_mod):
    bsz = c.shape[0]
    rows = -(-bsz // SUBLANES) * SUBLANES
    c_pad = jnp.zeros((rows, D_MODEL), F32).at[:bsz].set(c)
    out = pl.pallas_call(
        _mod_kernel,
        grid=(DEPTH,),
        in_specs=[
            pl.BlockSpec((rows, D_MODEL), lambda l: (0, 0)),
            pl.BlockSpec((None, D_MODEL, 3 * D_MODEL), lambda l: (l, 0, 0)),
            pl.BlockSpec((None, 1, 3 * D_MODEL), lambda l: (l, 0, 0)),
        ],
        out_specs=pl.BlockSpec((None, rows, 3 * D_MODEL), lambda l: (l, 0, 0)),
        out_shape=jax.ShapeDtypeStruct((DEPTH, rows, 3 * D_MODEL), F32),
        compiler_params=pltpu.CompilerParams(
            dimension_semantics=("arbitrary",), vmem_limit_bytes=VMEM_LIMIT_BYTES),
        name="adaln_modulation",
    )(c_pad, w_mod, b_mod.reshape(DEPTH, 1, 3 * D_MODEL))
    return out


def _mixer_kernel(n_batch, n_chunks, *refs):
    (x_cur, x_nxt,
     mod_ref, wz_ref, wx_ref, wbc_ref, wdt_ref, wsc_ref, gp_ref, hp_ref, convsc_ref, tri_ref, negmask_ref,
     eexp_ref, perm_ref,
     ynorm_ref, ybin_ref,
     u_ref, un_ref, tailx_ref, tailbc_ref, sctail_ref, tmp_ref, yout_ref, actx_ref, actbc_ref,
     xs_ref, xdtm_ref, xw_ref, m_ref, bt_ref, c_ref, hb_ref, hbn_ref, at_ref, atn_ref,
     ac_ref, acn_ref, state_ref) = refs
    L = CHUNK
    DT3, TE3, E3 = range(3)
    batch = pl.program_id(0)
    chunk = pl.program_id(1)
    seq_starts = chunk == 0
    last_chunk = chunk == n_chunks - 1
    nxt_batch = jnp.minimum(batch + last_chunk.astype(jnp.int32), n_batch - 1)
    lane = lax.broadcasted_iota(jnp.int32, (1, COL_BLOCK), 1)

    def norm_and_heads(x_ref, b):
        mod = mod_ref[pl.ds(b, 1), :]
        shift = mod[:, 0:D_MODEL]
        scale = mod[:, D_MODEL:2 * D_MODEL]
        u = (_layer_norm(x_ref[...]) * (1.0 + scale) + shift).astype(BF16)
        un_ref[...] = _dot(perm_ref[...], u).astype(BF16)
        dt = _softplus(_dot(un_ref[...], wdt_ref[...]) + hp_ref[HP_DT_BIAS:HP_DT_BIAS + 1, :])
        a = dt * (-jnp.exp(hp_ref[HP_A_LOG:HP_A_LOG + 1, :]))
        a_hi, a_mid, a_lo = _split3(a)
        tri = tri_ref[...]
        a_cum = (_dot(tri, a_hi.astype(BF16)) + _dot(tri, a_mid.astype(BF16))
                 + _dot(tri, a_lo.astype(BF16)))
        hbn_ref[DT3] = _lane_parts(dt)
        hbn_ref[TE3] = _lane_parts(jnp.exp(a_cum[L - 1:L, :] - a_cum))
        hbn_ref[E3] = _lane_parts(jnp.exp(a_cum))
        a_cum2 = a_cum * LOG2_E
        acn_ref[...] = a_cum2
        atn_ref[...] = a_cum2.T

    def stage_a(p, slot, u, fresh):
        gp = gp_ref.at[p]
        raw = _dot(u[...], wx_ref[p])
        actx_ref[slot] = _silu(_causal_conv(tmp_ref.at[2 * slot], raw, tailx_ref.at[p], gp, GP_CONVX,
                                            SSM_CONV, fresh) + gp[GP_CONVX_B:GP_CONVX_B + 1, :])
        raw = _dot(u[...], wbc_ref[p])
        actbc_ref[slot] = _silu(_causal_conv(tmp_ref.at[2 * slot + 1], raw, tailbc_ref.at[p], gp, GP_CONVBC,
                                             SSM_CONV, fresh) + gp[GP_CONVBC_B:GP_CONVBC_B + 1, :])

    def stage_b(p, src, slot, hb, at, ac):
        act = actx_ref[src]
        xs_ref[slot] = act
        xdt = act * _dot(hb[DT3], eexp_ref[p])
        for hh in range(HEADS_PER_GROUP):
            in_head = (lane >= hh * HEAD_DIM) & (lane < (hh + 1) * HEAD_DIM)
            xdtm_ref[slot, hh * L:(hh + 1) * L, :] = jnp.where(in_head, xdt, 0.0).astype(BF16)
        xw_ref[slot] = (xdt * _dot(hb[TE3], eexp_ref[p])).astype(BF16)
        act = actbc_ref[src]
        bmat = act[:, 0:STATE].astype(BF16)
        cmat = act[:, STATE:2 * STATE].astype(BF16)
        bt_ref[slot] = act[:, 0:STATE].T.astype(BF16)
        c_ref[slot] = cmat
        cb = lax.dot_general(cmat, bmat, (((1,), (1,)), ((), ())),
                             preferred_element_type=F32)
        for hh in range(HEADS_PER_GROUP):
            h = p * HEADS_PER_GROUP + hh
            arow = at[pl.ds(h, 1), :]
            acol = ac[:, h:h + 1]
            seg = acol - arow + negmask_ref[...]
            m_ref[slot, :, hh * L:(hh + 1) * L] = (cb * jnp.exp2(seg)).astype(BF16)

    def stage_c(g, slot):
        y = _dot(m_ref[slot], xdtm_ref[slot])
        state = jnp.where(seq_starts, 0.0, state_ref[g])
        escale = _dot(hb_ref[E3], eexp_ref[g])
        y = y + _dot(c_ref[slot], state.astype(BF16)) * escale
        state_ref[g] = state * escale[L - 1:L, :] + _dot(bt_ref[slot], xw_ref[slot])
        gp = gp_ref.at[g]
        y = y + gp[GP_D_SKIP:GP_D_SKIP + 1, :] * xs_ref[slot]
        y = y * _silu(_dot(u_ref[...], wz_ref[g]))
        y = y * lax.rsqrt(jnp.mean(y * y, axis=-1, keepdims=True) + RMS_EPS)
        _store_time_ordered(yout_ref.at[slot], y * gp[GP_NORM_W:GP_NORM_W + 1, :], ynorm_ref.at[g], slice(None))

    @pl.when((batch == 0) & (chunk == 0))
    def _():
        tailx_ref[...] = jnp.zeros_like(tailx_ref)
        tailbc_ref[...] = jnp.zeros_like(tailbc_ref)
        sctail_ref[...] = jnp.zeros_like(sctail_ref)
        state_ref[...] = jnp.zeros_like(state_ref)
        norm_and_heads(x_cur, batch)
        stage_a(0, 0, un_ref, True)
        stage_a(1, 1, un_ref, True)
        stage_b(0, 0, 0, hbn_ref, atn_ref, acn_ref)

    u_ref[...] = un_ref[...]
    hb_ref[...] = hbn_ref[...]
    at_ref[...] = atn_ref[...]
    ac_ref[...] = acn_ref[...]

    for g in range(0, GROUPS - 2, 2):
        stage_c(g, 0)
        stage_b(g + 1, 1, 1, hb_ref, at_ref, ac_ref)
        stage_a(g + 2, 0, u_ref, seq_starts)
        stage_c(g + 1, 1)
        stage_b(g + 2, 0, 0, hb_ref, at_ref, ac_ref)
        stage_a(g + 3, 1, u_ref, seq_starts)
    stage_c(GROUPS - 2, 0)
    stage_b(GROUPS - 1, 1, 1, hb_ref, at_ref, ac_ref)
    norm_and_heads(x_nxt, nxt_batch)
    stage_a(0, 0, un_ref, last_chunk)
    stage_c(GROUPS - 1, 1)
    stage_b(0, 0, 0, hbn_ref, atn_ref, acn_ref)
    stage_a(1, 1, un_ref, last_chunk)

    for blk in range(SC_WIDTH // COL_BLOCK):
        cols = slice(blk * COL_BLOCK, (blk + 1) * COL_BLOCK)
        u = u_ref[...]
        sc_b = _dot(u, wsc_ref[:, blk * COL_BLOCK:(blk + 1) * COL_BLOCK])
        sc_c = _dot(u, wsc_ref[:, SC_WIDTH + blk * COL_BLOCK:SC_WIDTH + (blk + 1) * COL_BLOCK])
        sc_x = _dot(u, wsc_ref[:, 2 * SC_WIDTH + blk * COL_BLOCK:2 * SC_WIDTH + (blk + 1) * COL_BLOCK])
        sc_g = _dot(u, wsc_ref[:, 3 * SC_WIDTH + blk * COL_BLOCK:3 * SC_WIDTH + (blk + 1) * COL_BLOCK])
        v = _causal_conv(tmp_ref.at[4 + blk % 2], sc_c * sc_x, sctail_ref.at[blk], convsc_ref.at[blk], 0,
                         SC_CONV, seq_starts)
        _store_time_ordered(yout_ref.at[2 + blk % 2], _silu(sc_g) * sc_b * v, ybin_ref, cols)


def _const(shape):
    return pl.BlockSpec(shape, lambda b, i: (0,) * len(shape), pipeline_mode=pl.Buffered(1))


def _layer_const(layer, shape):
    return pl.BlockSpec((None,) + tuple(shape[1:]), lambda b, i: (layer,) + (0,) * (len(shape) - 1),
                        pipeline_mode=pl.Buffered(1))


def _mixer(layer, x, layer_consts, consts):
    bsz, seq, _ = x.shape
    L = CHUNK
    assert seq % L == 0
    n_chunks = seq // L

    def next_block(b, i):
        wraps = (i == n_chunks - 1).astype(jnp.int32)
        return (jnp.minimum(b + wraps, bsz - 1), (i + 1) * (1 - wraps), 0)

    return pl.pallas_call(
        functools.partial(_mixer_kernel, bsz, n_chunks),
        grid=(bsz, n_chunks),
        in_specs=[
            pl.BlockSpec((None, L, D_MODEL), lambda b, i: (b, i, 0)),
            pl.BlockSpec((None, L, D_MODEL), next_block),
        ] + [_layer_const(layer, a.shape) for a in layer_consts] + [_const(a.shape) for a in consts],
        out_specs=[
            pl.BlockSpec((None, GROUPS, L, GROUP_WIDTH), lambda b, i: (b, 0, i, 0)),
            pl.BlockSpec((None, L, SC_WIDTH), lambda b, i: (b, i, 0)),
        ],
        out_shape=[
            jax.ShapeDtypeStruct((bsz, GROUPS, seq, GROUP_WIDTH), BF16),
            jax.ShapeDtypeStruct((bsz, seq, SC_WIDTH), BF16),
        ],
        scratch_shapes=[
            pltpu.VMEM((L, D_MODEL), BF16),
            pltpu.VMEM((L, D_MODEL), BF16),
            pltpu.VMEM((GROUPS, SSM_HALO, GROUP_WIDTH), F32),
            pltpu.VMEM((GROUPS, SSM_HALO, 2 * STATE), F32),
            pltpu.VMEM((SC_WIDTH // COL_BLOCK, SC_HALO, COL_BLOCK), F32),
            pltpu.VMEM((6, SSM_HALO + L, COL_BLOCK), F32),
            pltpu.VMEM((4, COL_BLOCK // LANES, L, LANES), F32),
            pltpu.VMEM((2, L, GROUP_WIDTH), F32),
            pltpu.VMEM((2, L, 2 * STATE), F32),
            pltpu.VMEM((2, L, GROUP_WIDTH), F32),
            pltpu.VMEM((2, HEADS_PER_GROUP * L, GROUP_WIDTH), BF16),
            pltpu.VMEM((2, L, GROUP_WIDTH), BF16),
            pltpu.VMEM((2, L, HEADS_PER_GROUP * L), BF16),
            pltpu.VMEM((2, STATE, L), BF16),
            pltpu.VMEM((2, L, STATE), BF16),
            pltpu.VMEM((3, L, LANES), BF16),
            pltpu.VMEM((3, L, LANES), BF16),
            pltpu.VMEM((LANES, L), F32),
            pltpu.VMEM((LANES, L), F32),
            pltpu.VMEM((L, LANES), F32),
            pltpu.VMEM((L, LANES), F32),
            pltpu.VMEM((GROUPS, STATE, GROUP_WIDTH), F32),
        ],
        compiler_params=pltpu.CompilerParams(
            dimension_semantics=("arbitrary", "arbitrary"), vmem_limit_bytes=VMEM_LIMIT_BYTES),
        name="ssd_shortconv_mixer",
    )(x, x, *layer_consts, *consts)


def _merge_kernel(x_ref, ynorm_ref, ybin_ref, mod_ref, wgate_ref, bgate_ref, wa_ref, wb_ref,
                  wout_ref, ln_ref, o_ref):
    x = x_ref[...]
    mod = mod_ref[pl.ds(pl.program_id(0), 1), :]
    shift = mod[:, 0:D_MODEL]
    scale = mod[:, D_MODEL:2 * D_MODEL]
    gate = mod[:, 2 * D_MODEL:3 * D_MODEL]
    u = (_layer_norm(x) * (1.0 + scale) + shift).astype(BF16)
    gates = jax.nn.sigmoid(_dot(u, wgate_ref[...]) + bgate_ref[...])
    y_a = _dot(ynorm_ref[0], wa_ref[0:GROUP_WIDTH, :])
    for g in range(1, GROUPS):
        y_a = y_a + _dot(ynorm_ref[g], wa_ref[g * GROUP_WIDTH:(g + 1) * GROUP_WIDTH, :])
    y_b = _dot(ybin_ref[...], wb_ref[...])
    merged = gates[:, 0:D_MODEL] * y_a + gates[:, D_MODEL:2 * D_MODEL] * y_b
    mixed = _dot(merged.astype(BF16), wout_ref[...])
    o_ref[...] = _layer_norm(DN_ALPHA * x + gate * mixed) * ln_ref[0:1, :] + ln_ref[1:2, :]


def _merge(layer, x, ynorm, ybin, layer_consts):
    bsz, seq, _ = x.shape
    tm = min(MERGE_TILE, seq)
    assert seq % tm == 0
    return pl.pallas_call(
        _merge_kernel,
        grid=(bsz, seq // tm),
        in_specs=[
            pl.BlockSpec((None, tm, D_MODEL), lambda b, i: (b, i, 0)),
            pl.BlockSpec((None, GROUPS, tm, GROUP_WIDTH), lambda b, i: (b, 0, i, 0)),
            pl.BlockSpec((None, tm, SC_WIDTH), lambda b, i: (b, i, 0)),
        ] + [_layer_const(layer, a.shape) for a in layer_consts],
        out_specs=pl.BlockSpec((None, tm, D_MODEL), lambda b, i: (b, i, 0)),
        out_shape=jax.ShapeDtypeStruct((bsz, seq, D_MODEL), F32),
        compiler_params=pltpu.CompilerParams(
            dimension_semantics=("arbitrary", "arbitrary"), vmem_limit_bytes=VMEM_LIMIT_BYTES),
        name="gated_merge",
    )(x, ynorm, ybin, *layer_consts)


def _head_lanes(v):
    pad = jnp.zeros((v.shape[0], LANES - HEAD_REPLICAS * HEADS), F32)
    return jnp.concatenate([v] * HEAD_REPLICAS + [pad], axis=1)


def _selection_constants():
    L = CHUNK
    idx = jnp.arange(L)
    time = ROW_TILES * (idx % SUBLANES) + idx // SUBLANES
    causal = time[:, None] >= time[None, :]
    tri = causal.astype(BF16)
    negmask = jnp.where(causal, 0.0, -jnp.inf).astype(F32)
    j = jnp.arange(LANES)
    head_of_lane = jnp.where(j < HEAD_REPLICAS * HEADS, j % HEADS, -1)
    ch_head = jnp.arange(D_INNER) // HEAD_DIM
    eexp = (head_of_lane[:, None] == ch_head[None, :]).astype(BF16)
    eexp = eexp.reshape(LANES, GROUPS, GROUP_WIDTH).transpose(1, 0, 2)
    perm = (time[:, None] == idx[None, :]).astype(BF16)
    return tri, negmask, eexp, perm


def _by_group(a, width):
    return a.reshape(a.shape[0], a.shape[1], GROUPS, width).transpose(0, 2, 1, 3)


def _bc_by_group(a):
    b = a[..., D_INNER:D_INNER + GROUPS * STATE].reshape(a.shape[0], a.shape[1], GROUPS, STATE)
    c = a[..., D_INNER + GROUPS * STATE:].reshape(a.shape[0], a.shape[1], GROUPS, STATE)
    return jnp.concatenate([b, c], axis=3).transpose(0, 2, 1, 3)


def kernel(x, c, w_mod, b_mod, w_in, conv_ssm_w, conv_ssm_b, dt_bias, a_log, d_skip, ssm_norm_w,
           conv_sc_w, w_branch_a, w_branch_b, w_gate, b_gate, w_out, ln_g, ln_b):
    mod = _modulation(c, w_mod, b_mod)
    wz = _by_group(w_in[:, :, :SPLIT_Z].astype(BF16), GROUP_WIDTH)
    wx = _by_group(w_in[:, :, SPLIT_Z:SPLIT_Z + D_INNER].astype(BF16), GROUP_WIDTH)
    wbc = _bc_by_group(w_in[:, :, SPLIT_Z:SPLIT_XBC].astype(BF16))
    wdt_h = w_in[:, :, SPLIT_XBC:SPLIT_DT].astype(BF16)
    wdt = jnp.concatenate(
        [wdt_h] * HEAD_REPLICAS + [jnp.zeros((DEPTH, D_MODEL, LANES - HEAD_REPLICAS * HEADS), BF16)], axis=2)
    wsc = w_in[:, :, SPLIT_DT:].astype(BF16)
    conv_b = conv_ssm_b.reshape(DEPTH, 1, XBC)
    group_params = jnp.concatenate([
        _by_group(conv_ssm_w[:, :, :D_INNER], GROUP_WIDTH), _by_group(conv_b[:, :, :D_INNER], GROUP_WIDTH),
        _bc_by_group(conv_ssm_w), _bc_by_group(conv_b),
        jnp.repeat(d_skip, HEAD_DIM, axis=1).reshape(DEPTH, GROUPS, 1, GROUP_WIDTH),
        ssm_norm_w.reshape(DEPTH, GROUPS, 1, GROUP_WIDTH)], axis=2)
    head_params = jnp.stack([_head_lanes(dt_bias), _head_lanes(a_log)], axis=1)
    conv_sc = conv_sc_w.reshape(DEPTH, SC_CONV, SC_WIDTH // COL_BLOCK, COL_BLOCK).transpose(0, 2, 1, 3)
    mixer_consts = (mod, wz, wx, wbc, wdt, wsc, group_params, head_params, conv_sc)
    merge_consts = (mod, w_gate.astype(BF16), b_gate.reshape(DEPTH, 1, 2 * D_MODEL), w_branch_a.astype(BF16),
                    w_branch_b.astype(BF16), w_out.astype(BF16), jnp.stack([ln_g, ln_b], axis=1))
    shared = _selection_constants()
    for l in range(DEPTH):
        ynorm, ybin = _mixer(l, x, mixer_consts, shared)
        x = _merge(l, x, ynorm, ybin, merge_consts)
    return x
```

```python
import functools

import jax
import jax.numpy as jnp
from jax import lax
from jax.experimental import pallas as pl
from jax.experimental.pallas import tpu as pltpu

D_MODEL = 1024
DEPTH = 2
D_INNER = 2048
HEAD_DIM = 64
HEADS = 32
GROUPS = 8
STATE = 128
SSM_CONV = 4
CHUNK = 256
XBC = D_INNER + 2 * GROUPS * STATE
SC_WIDTH = 1024
SC_CONV = 3
HEADS_PER_GROUP = HEADS // GROUPS
GROUP_WIDTH = HEADS_PER_GROUP * HEAD_DIM
SPLIT_Z = D_INNER
SPLIT_XBC = SPLIT_Z + XBC
SPLIT_DT = SPLIT_XBC + HEADS
DN_ALPHA = (2 * DEPTH) ** 0.25
LN_EPS = 1e-5
RMS_EPS = 1e-5
LOG2_E = 1.4426950408889634

LANES = 128
SUBLANES = 8
ROW_TILES = CHUNK // SUBLANES
SSM_HALO = SUBLANES * (SSM_CONV - 1)
SC_HALO = SUBLANES * (SC_CONV - 1)
COL_BLOCK = 256
HEAD_REPLICAS = 3
GP_CONVX, GP_CONVX_B, GP_CONVBC, GP_CONVBC_B, GP_D_SKIP, GP_NORM_W = 0, 4, 5, 9, 10, 11
HP_DT_BIAS, HP_A_LOG = 0, 1
MERGE_TILE = 512
VMEM_LIMIT_BYTES = 56 * 1024 * 1024

F32 = jnp.float32
BF16 = jnp.bfloat16


def _dot(a, b):
    return jnp.dot(a, b, preferred_element_type=F32)


def _layer_norm(x):
    mu = jnp.mean(x, axis=-1, keepdims=True)
    xc = x - mu
    var = jnp.mean(xc * xc, axis=-1, keepdims=True)
    return xc * lax.rsqrt(var + LN_EPS)


def _silu(x):
    h = 0.5 * x
    return h + h * jnp.tanh(h)


def _softplus(x):
    return jnp.maximum(x, 0.0) + jnp.log1p(jnp.exp(-jnp.abs(x)))


def _split3(x):
    hi = x.astype(BF16).astype(F32)
    r = x - hi
    mid = r.astype(BF16).astype(F32)
    return hi, mid, r - mid


def _lane_parts(x):
    hi, mid, lo = _split3(x)
    lane = lax.broadcasted_iota(jnp.int32, x.shape, 1)
    return jnp.where(lane < HEADS, hi, jnp.where(lane < 2 * HEADS, mid, lo)).astype(BF16)


def _store_time_ordered(stage_ref, value, out_ref, cols):
    for j in range(value.shape[1] // LANES):
        stage_ref[j] = value[:, j * LANES:(j + 1) * LANES]
    out_ref[:, cols] = jnp.concatenate(
        [jnp.concatenate([stage_ref.at[j][pl.ds(s, ROW_TILES, stride=SUBLANES), :]
                          for s in range(SUBLANES)], axis=0)
         for j in range(value.shape[1] // LANES)], axis=1).astype(out_ref.dtype)


def _conv_halo(prev_tail, cur_tail):
    rows = cur_tail.shape[0]
    sub = lax.broadcasted_iota(jnp.int32, cur_tail.shape, 0) % SUBLANES
    return jnp.where(sub == 0,
                     pltpu.roll(prev_tail, rows - (SUBLANES - 1), axis=0),
                     pltpu.roll(cur_tail, 1, axis=0))


def _causal_conv(buf, raw, tail, w, row0, taps, fresh):
    halo = SUBLANES * (taps - 1)
    cur_tail = raw[CHUNK - halo:CHUNK, :]
    buf[0:halo, :] = _conv_halo(jnp.where(fresh, 0.0, tail[...]), cur_tail)
    buf[halo:halo + CHUNK, :] = raw
    tail[...] = cur_tail
    acc = w[row0 + taps - 1:row0 + taps, :] * raw
    for k in range(taps - 1):
        acc = acc + w[row0 + k:row0 + k + 1, :] * buf[pl.ds(SUBLANES * k, CHUNK), :]
    return acc


def _mod_kernel(c_ref, w_ref, b_ref, o_ref):
    o_ref[...] = jnp.dot(c_ref[...], w_ref[...], precision=lax.Precision.HIGHEST,
                         preferred_element_type=F32) + b_ref[...]


def _modulation(c, w_mod, b_mod):
    bsz = c.shape[0]
    rows = -(-bsz // SUBLANES) * SUBLANES
    c_pad = jnp.zeros((rows, D_MODEL), F32).at[:bsz].set(c)
    out = pl.pallas_call(
        _mod_kernel,
        grid=(DEPTH,),
        in_specs=[
            pl.BlockSpec((rows, D_MODEL), lambda l: (0, 0)),
            pl.BlockSpec((None, D_MODEL, 3 * D_MODEL), lambda l: (l, 0, 0)),
            pl.BlockSpec((None, 1, 3 * D_MODEL), lambda l: (l, 0, 0)),
        ],
        out_specs=pl.BlockSpec((None, rows, 3 * D_MODEL), lambda l: (l, 0, 0)),
        out_shape=jax.ShapeDtypeStruct((DEPTH, rows, 3 * D_MODEL), F32),
        compiler_params=pltpu.CompilerParams(
            dimension_semantics=("arbitrary",), vmem_limit_bytes=VMEM_LIMIT_BYTES),
        name="adaln_modulation",
    )(c_pad, w_mod, b_mod.reshape(DEPTH, 1, 3 * D_MODEL))
    return out


def _mixer_kernel(n_batch, n_chunks, *refs):
    (x_cur, x_nxt,
     mod_ref, wz_ref, wx_ref, wbc_ref, wdt_ref, wsc_ref, gp_ref, hp_ref, convsc_ref, tri_ref, negmask_t_ref,
     eexp_ref, perm_ref,
     ynorm_ref, ybin_ref,
     u_ref, un_ref, tailx_ref, tailbc_ref, sctail_ref, tmp_ref, yout_ref, actx_ref, actbc_ref,
     xs_ref, xdtt_ref, xw_ref, mt_ref, bt_ref, c_ref, hb_ref, hbn_ref, at_ref, atn_ref,
     ac_ref, acn_ref, state_ref) = refs
    L = CHUNK
    DT3, TE3, E3 = range(3)
    batch = pl.program_id(0)
    chunk = pl.program_id(1)
    seq_starts = chunk == 0
    last_chunk = chunk == n_chunks - 1
    nxt_batch = jnp.minimum(batch + last_chunk.astype(jnp.int32), n_batch - 1)

    def norm_and_heads(x_ref, b):
        mod = mod_ref[pl.ds(b, 1), :]
        shift = mod[:, 0:D_MODEL]
        scale = mod[:, D_MODEL:2 * D_MODEL]
        u = (_layer_norm(x_ref[...]) * (1.0 + scale) + shift).astype(BF16)
        un_ref[...] = _dot(perm_ref[...], u).astype(BF16)
        dt = _softplus(_dot(un_ref[...], wdt_ref[...]) + hp_ref[HP_DT_BIAS:HP_DT_BIAS + 1, :])
        a = dt * (-jnp.exp(hp_ref[HP_A_LOG:HP_A_LOG + 1, :]))
        a_hi, a_mid, a_lo = _split3(a)
        tri = tri_ref[...]
        a_cum = (_dot(tri, a_hi.astype(BF16)) + _dot(tri, a_mid.astype(BF16))
                 + _dot(tri, a_lo.astype(BF16)))
        hbn_ref[DT3] = _lane_parts(dt)
        hbn_ref[TE3] = _lane_parts(jnp.exp(a_cum[L - 1:L, :] - a_cum))
        hbn_ref[E3] = _lane_parts(jnp.exp(a_cum))
        a_cum2 = a_cum * LOG2_E
        acn_ref[...] = a_cum2
        atn_ref[...] = a_cum2.T

    def stage_a(p, slot, u, fresh):
        gp = gp_ref.at[p]
        raw = _dot(u[...], wx_ref[p])
        actx_ref[slot] = _silu(_causal_conv(tmp_ref.at[2 * slot], raw, tailx_ref.at[p], gp, GP_CONVX,
                                            SSM_CONV, fresh) + gp[GP_CONVX_B:GP_CONVX_B + 1, :])
        raw = _dot(u[...], wbc_ref[p])
        actbc_ref[slot] = _silu(_causal_conv(tmp_ref.at[2 * slot + 1], raw, tailbc_ref.at[p], gp, GP_CONVBC,
                                             SSM_CONV, fresh) + gp[GP_CONVBC_B:GP_CONVBC_B + 1, :])

    def stage_b(p, src, slot, hb, at, ac):
        act = actx_ref[src]
        xs_ref[slot] = act
        xdt = act * _dot(hb[DT3], eexp_ref[p])
        xdtt_ref[slot] = xdt.T.astype(BF16)
        xw_ref[slot] = (xdt * _dot(hb[TE3], eexp_ref[p])).astype(BF16)
        act = actbc_ref[src]
        bmat = act[:, 0:STATE].astype(BF16)
        cmat = act[:, STATE:2 * STATE].astype(BF16)
        bt_ref[slot] = act[:, 0:STATE].T.astype(BF16)
        c_ref[slot] = cmat
        bc = lax.dot_general(bmat, cmat, (((1,), (1,)), ((), ())),
                             preferred_element_type=F32)
        for hh in range(HEADS_PER_GROUP):
            h = p * HEADS_PER_GROUP + hh
            a_l = at[pl.ds(h, 1), :]
            a_s = ac[:, h:h + 1]
            seg = a_l - a_s + negmask_t_ref[...]
            mt_ref[slot, hh] = (bc * jnp.exp2(seg)).astype(BF16)

    def stage_c(g, slot):
        y = jnp.concatenate(
            [_dot(xdtt_ref[slot, hh * HEAD_DIM:(hh + 1) * HEAD_DIM, :], mt_ref[slot, hh])
             for hh in range(HEADS_PER_GROUP)], axis=0).T
        state = jnp.where(seq_starts, 0.0, state_ref[g])
        escale = _dot(hb_ref[E3], eexp_ref[g])
        y = y + _dot(c_ref[slot], state.astype(BF16)) * escale
        state_ref[g] = state * escale[L - 1:L, :] + _dot(bt_ref[slot], xw_ref[slot])
        gp = gp_ref.at[g]
        y = y + gp[GP_D_SKIP:GP_D_SKIP + 1, :] * xs_ref[slot]
        y = y * _silu(_dot(u_ref[...], wz_ref[g]))
        y = y * lax.rsqrt(jnp.mean(y * y, axis=-1, keepdims=True) + RMS_EPS)
        _store_time_ordered(yout_ref.at[slot], y * gp[GP_NORM_W:GP_NORM_W + 1, :], ynorm_ref.at[g], slice(None))

    @pl.when((batch == 0) & (chunk == 0))
    def _():
        tailx_ref[...] = jnp.zeros_like(tailx_ref)
        tailbc_ref[...] = jnp.zeros_like(tailbc_ref)
        sctail_ref[...] = jnp.zeros_like(sctail_ref)
        state_ref[...] = jnp.zeros_like(state_ref)
        norm_and_heads(x_cur, batch)
        stage_a(0, 0, un_ref, True)
        stage_a(1, 1, un_ref, True)
        stage_b(0, 0, 0, hbn_ref, atn_ref, acn_ref)

    u_ref[...] = un_ref[...]
    hb_ref[...] = hbn_ref[...]
    at_ref[...] = atn_ref[...]
    ac_ref[...] = acn_ref[...]

    for g in range(0, GROUPS - 2, 2):
        stage_c(g, 0)
        stage_b(g + 1, 1, 1, hb_ref, at_ref, ac_ref)
        stage_a(g + 2, 0, u_ref, seq_starts)
        stage_c(g + 1, 1)
        stage_b(g + 2, 0, 0, hb_ref, at_ref, ac_ref)
        stage_a(g + 3, 1, u_ref, seq_starts)
    stage_c(GROUPS - 2, 0)
    stage_b(GROUPS - 1, 1, 1, hb_ref, at_ref, ac_ref)
    norm_and_heads(x_nxt, nxt_batch)
    stage_a(0, 0, un_ref, last_chunk)
    stage_c(GROUPS - 1, 1)
    stage_b(0, 0, 0, hbn_ref, atn_ref, acn_ref)
    stage_a(1, 1, un_ref, last_chunk)

    for blk in range(SC_WIDTH // COL_BLOCK):
        cols = slice(blk * COL_BLOCK, (blk + 1) * COL_BLOCK)
        u = u_ref[...]
        sc_b = _dot(u, wsc_ref[:, blk * COL_BLOCK:(blk + 1) * COL_BLOCK])
        sc_c = _dot(u, wsc_ref[:, SC_WIDTH + blk * COL_BLOCK:SC_WIDTH + (blk + 1) * COL_BLOCK])
        sc_x = _dot(u, wsc_ref[:, 2 * SC_WIDTH + blk * COL_BLOCK:2 * SC_WIDTH + (blk + 1) * COL_BLOCK])
        sc_g = _dot(u, wsc_ref[:, 3 * SC_WIDTH + blk * COL_BLOCK:3 * SC_WIDTH + (blk + 1) * COL_BLOCK])
        v = _causal_conv(tmp_ref.at[4 + blk % 2], sc_c * sc_x, sctail_ref.at[blk], convsc_ref.at[blk], 0,
                         SC_CONV, seq_starts)
        _store_time_ordered(yout_ref.at[2 + blk % 2], _silu(sc_g) * sc_b * v, ybin_ref, cols)


def _const(shape):
    return pl.BlockSpec(shape, lambda b, i: (0,) * len(shape), pipeline_mode=pl.Buffered(1))


def _layer_const(layer, shape):
    return pl.BlockSpec((None,) + tuple(shape[1:]), lambda b, i: (layer,) + (0,) * (len(shape) - 1),
                        pipeline_mode=pl.Buffered(1))


def _mixer(layer, x, layer_consts, consts):
    bsz, seq, _ = x.shape
    L = CHUNK
    assert seq % L == 0
    n_chunks = seq // L

    def next_block(b, i):
        wraps = (i == n_chunks - 1).astype(jnp.int32)
        return (jnp.minimum(b + wraps, bsz - 1), (i + 1) * (1 - wraps), 0)

    return pl.pallas_call(
        functools.partial(_mixer_kernel, bsz, n_chunks),
        grid=(bsz, n_chunks),
        in_specs=[
            pl.BlockSpec((None, L, D_MODEL), lambda b, i: (b, i, 0)),
            pl.BlockSpec((None, L, D_MODEL), next_block),
        ] + [_layer_const(layer, a.shape) for a in layer_consts] + [_const(a.shape) for a in consts],
        out_specs=[
            pl.BlockSpec((None, GROUPS, L, GROUP_WIDTH), lambda b, i: (b, 0, i, 0)),
            pl.BlockSpec((None, L, SC_WIDTH), lambda b, i: (b, i, 0)),
        ],
        out_shape=[
            jax.ShapeDtypeStruct((bsz, GROUPS, seq, GROUP_WIDTH), BF16),
            jax.ShapeDtypeStruct((bsz, seq, SC_WIDTH), BF16),
        ],
        scratch_shapes=[
            pltpu.VMEM((L, D_MODEL), BF16),
            pltpu.VMEM((L, D_MODEL), BF16),
            pltpu.VMEM((GROUPS, SSM_HALO, GROUP_WIDTH), F32),
            pltpu.VMEM((GROUPS, SSM_HALO, 2 * STATE), F32),
            pltpu.VMEM((SC_WIDTH // COL_BLOCK, SC_HALO, COL_BLOCK), F32),
            pltpu.VMEM((6, SSM_HALO + L, COL_BLOCK), F32),
            pltpu.VMEM((4, COL_BLOCK // LANES, L, LANES), F32),
            pltpu.VMEM((2, L, GROUP_WIDTH), F32),
            pltpu.VMEM((2, L, 2 * STATE), F32),
            pltpu.VMEM((2, L, GROUP_WIDTH), F32),
            pltpu.VMEM((2, GROUP_WIDTH, L), BF16),
            pltpu.VMEM((2, L, GROUP_WIDTH), BF16),
            pltpu.VMEM((2, HEADS_PER_GROUP, L, L), BF16),
            pltpu.VMEM((2, STATE, L), BF16),
            pltpu.VMEM((2, L, STATE), BF16),
            pltpu.VMEM((3, L, LANES), BF16),
            pltpu.VMEM((3, L, LANES), BF16),
            pltpu.VMEM((LANES, L), F32),
            pltpu.VMEM((LANES, L), F32),
            pltpu.VMEM((L, LANES), F32),
            pltpu.VMEM((L, LANES), F32),
            pltpu.VMEM((GROUPS, STATE, GROUP_WIDTH), F32),
        ],
        compiler_params=pltpu.CompilerParams(
            dimension_semantics=("arbitrary", "arbitrary"), vmem_limit_bytes=VMEM_LIMIT_BYTES),
        name="ssd_shortconv_mixer",
    )(x, x, *layer_consts, *consts)


def _merge_kernel(x_ref, ynorm_ref, ybin_ref, mod_ref, wgate_ref, bgate_ref, wa_ref, wb_ref,
                  wout_ref, ln_ref, o_ref):
    x = x_ref[...]
    mod = mod_ref[pl.ds(pl.program_id(0), 1), :]
    shift = mod[:, 0:D_MODEL]
    scale = mod[:, D_MODEL:2 * D_MODEL]
    gate = mod[:, 2 * D_MODEL:3 * D_MODEL]
    u = (_layer_norm(x) * (1.0 + scale) + shift).astype(BF16)
    gates = jax.nn.sigmoid(_dot(u, wgate_ref[...]) + bgate_ref[...])
    y_a = _dot(ynorm_ref[0], wa_ref[0:GROUP_WIDTH, :])
    for g in range(1, GROUPS):
        y_a = y_a + _dot(ynorm_ref[g], wa_ref[g * GROUP_WIDTH:(g + 1) * GROUP_WIDTH, :])
    y_b = _dot(ybin_ref[...], wb_ref[...])
    merged = gates[:, 0:D_MODEL] * y_a + gates[:, D_MODEL:2 * D_MODEL] * y_b
    mixed = _dot(merged.astype(BF16), wout_ref[...])
    o_ref[...] = _layer_norm(DN_ALPHA * x + gate * mixed) * ln_ref[0:1, :] + ln_ref[1:2, :]


def _merge(layer, x, ynorm, ybin, layer_consts):
    bsz, seq, _ = x.shape
    tm = min(MERGE_TILE, seq)
    assert seq % tm == 0
    return pl.pallas_call(
        _merge_kernel,
        grid=(bsz, seq // tm),
        in_specs=[
            pl.BlockSpec((None, tm, D_MODEL), lambda b, i: (b, i, 0)),
            pl.BlockSpec((None, GROUPS, tm, GROUP_WIDTH), lambda b, i: (b, 0, i, 0)),
            pl.BlockSpec((None, tm, SC_WIDTH), lambda b, i: (b, i, 0)),
        ] + [_layer_const(layer, a.shape) for a in layer_consts],
        out_specs=pl.BlockSpec((None, tm, D_MODEL), lambda b, i: (b, i, 0)),
        out_shape=jax.ShapeDtypeStruct((bsz, seq, D_MODEL), F32),
        compiler_params=pltpu.CompilerParams(
            dimension_semantics=("arbitrary", "arbitrary"), vmem_limit_bytes=VMEM_LIMIT_BYTES),
        name="gated_merge",
    )(x, ynorm, ybin, *layer_consts)


def _head_lanes(v):
    pad = jnp.zeros((v.shape[0], LANES - HEAD_REPLICAS * HEADS), F32)
    return jnp.concatenate([v] * HEAD_REPLICAS + [pad], axis=1)


def _selection_constants():
    L = CHUNK
    idx = jnp.arange(L)
    time = ROW_TILES * (idx % SUBLANES) + idx // SUBLANES
    causal = time[:, None] >= time[None, :]
    tri = causal.astype(BF16)
    negmask_t = jnp.where(causal.T, 0.0, -jnp.inf).astype(F32)
    j = jnp.arange(LANES)
    head_of_lane = jnp.where(j < HEAD_REPLICAS * HEADS, j % HEADS, -1)
    ch_head = jnp.arange(D_INNER) // HEAD_DIM
    eexp = (head_of_lane[:, None] == ch_head[None, :]).astype(BF16)
    eexp = eexp.reshape(LANES, GROUPS, GROUP_WIDTH).transpose(1, 0, 2)
    perm = (time[:, None] == idx[None, :]).astype(BF16)
    return tri, negmask_t, eexp, perm


def _by_group(a, width):
    return a.reshape(a.shape[0], a.shape[1], GROUPS, width).transpose(0, 2, 1, 3)


def _bc_by_group(a):
    b = a[..., D_INNER:D_INNER + GROUPS * STATE].reshape(a.shape[0], a.shape[1], GROUPS, STATE)
    c = a[..., D_INNER + GROUPS * STATE:].reshape(a.shape[0], a.shape[1], GROUPS, STATE)
    return jnp.concatenate([b, c], axis=3).transpose(0, 2, 1, 3)


def kernel(x, c, w_mod, b_mod, w_in, conv_ssm_w, conv_ssm_b, dt_bias, a_log, d_skip, ssm_norm_w,
           conv_sc_w, w_branch_a, w_branch_b, w_gate, b_gate, w_out, ln_g, ln_b):
    mod = _modulation(c, w_mod, b_mod)
    wz = _by_group(w_in[:, :, :SPLIT_Z].astype(BF16), GROUP_WIDTH)
    wx = _by_group(w_in[:, :, SPLIT_Z:SPLIT_Z + D_INNER].astype(BF16), GROUP_WIDTH)
    wbc = _bc_by_group(w_in[:, :, SPLIT_Z:SPLIT_XBC].astype(BF16))
    wdt_h = w_in[:, :, SPLIT_XBC:SPLIT_DT].astype(BF16)
    wdt = jnp.concatenate(
        [wdt_h] * HEAD_REPLICAS + [jnp.zeros((DEPTH, D_MODEL, LANES - HEAD_REPLICAS * HEADS), BF16)], axis=2)
    wsc = w_in[:, :, SPLIT_DT:].astype(BF16)
    conv_b = conv_ssm_b.reshape(DEPTH, 1, XBC)
    group_params = jnp.concatenate([
        _by_group(conv_ssm_w[:, :, :D_INNER], GROUP_WIDTH), _by_group(conv_b[:, :, :D_INNER], GROUP_WIDTH),
        _bc_by_group(conv_ssm_w), _bc_by_group(conv_b),
        jnp.repeat(d_skip, HEAD_DIM, axis=1).reshape(DEPTH, GROUPS, 1, GROUP_WIDTH),
        ssm_norm_w.reshape(DEPTH, GROUPS, 1, GROUP_WIDTH)], axis=2)
    head_params = jnp.stack([_head_lanes(dt_bias), _head_lanes(a_log)], axis=1)
    conv_sc = conv_sc_w.reshape(DEPTH, SC_CONV, SC_WIDTH // COL_BLOCK, COL_BLOCK).transpose(0, 2, 1, 3)
    mixer_consts = (mod, wz, wx, wbc, wdt, wsc, group_params, head_params, conv_sc)
    merge_consts = (mod, w_gate.astype(BF16), b_gate.reshape(DEPTH, 1, 2 * D_MODEL), w_branch_a.astype(BF16),
                    w_branch_b.astype(BF16), w_out.astype(BF16), jnp.stack([ln_g, ln_b], axis=1))
    shared = _selection_constants()
    for l in range(DEPTH):
        ynorm, ybin = _mixer(l, x, mixer_consts, shared)
        x = _merge(l, x, ynorm, ybin, merge_consts)
    return x
```

```python
import functools

import jax
import jax.numpy as jnp
from jax import lax
from jax.experimental import pallas as pl
from jax.experimental.pallas import tpu as pltpu

D_MODEL = 1024
DEPTH = 2
D_INNER = 2048
HEAD_DIM = 64
HEADS = 32
GROUPS = 8
STATE = 128
SSM_CONV = 4
CHUNK = 256
XBC = D_INNER + 2 * GROUPS * STATE
SC_WIDTH = 1024
SC_CONV = 3
HEADS_PER_GROUP = HEADS // GROUPS
GROUP_WIDTH = HEADS_PER_GROUP * HEAD_DIM
SPLIT_Z = D_INNER
SPLIT_XBC = SPLIT_Z + XBC
SPLIT_DT = SPLIT_XBC + HEADS
DN_ALPHA = (2 * DEPTH) ** 0.25
LN_EPS = 1e-5
RMS_EPS = 1e-5
LOG2_E = 1.4426950408889634

LANES = 128
SUBLANES = 8
ROW_TILES = CHUNK // SUBLANES
SSM_HALO = SUBLANES * (SSM_CONV - 1)
SC_HALO = SUBLANES * (SC_CONV - 1)
COL_BLOCK = 256
HEAD_REPLICAS = 3
GP_CONVX, GP_CONVX_B, GP_CONVBC, GP_CONVBC_B, GP_D_SKIP, GP_NORM_W = 0, 4, 5, 9, 10, 11
HP_DT_BIAS, HP_A_LOG = 0, 1
MERGE_TILE = 512
VMEM_LIMIT_BYTES = 56 * 1024 * 1024

F32 = jnp.float32
BF16 = jnp.bfloat16


def _dot(a, b):
    return jnp.dot(a, b, preferred_element_type=F32)


def _layer_norm(x):
    mu = jnp.mean(x, axis=-1, keepdims=True)
    xc = x - mu
    var = jnp.mean(xc * xc, axis=-1, keepdims=True)
    return xc * lax.rsqrt(var + LN_EPS)


def _silu(x):
    h = 0.5 * x
    return h + h * jnp.tanh(h)


def _softplus(x):
    return jnp.maximum(x, 0.0) + jnp.log1p(jnp.exp(-jnp.abs(x)))


def _split3(x):
    hi = x.astype(BF16).astype(F32)
    r = x - hi
    mid = r.astype(BF16).astype(F32)
    return hi, mid, r - mid


def _store_time_ordered(stage_ref, value, out_ref, cols):
    for j in range(value.shape[1] // LANES):
        stage_ref[j] = value[:, j * LANES:(j + 1) * LANES]
    out_ref[:, cols] = jnp.concatenate(
        [jnp.concatenate([stage_ref.at[j][pl.ds(s, ROW_TILES, stride=SUBLANES), :]
                          for s in range(SUBLANES)], axis=0)
         for j in range(value.shape[1] // LANES)], axis=1).astype(out_ref.dtype)


def _conv_halo(prev_tail, cur_tail):
    rows = cur_tail.shape[0]
    sub = lax.broadcasted_iota(jnp.int32, cur_tail.shape, 0) % SUBLANES
    return jnp.where(sub == 0,
                     pltpu.roll(prev_tail, rows - (SUBLANES - 1), axis=0),
                     pltpu.roll(cur_tail, 1, axis=0))


def _causal_conv(buf, raw, tail, w, row0, taps, fresh):
    halo = SUBLANES * (taps - 1)
    cur_tail = raw[CHUNK - halo:CHUNK, :]
    buf[0:halo, :] = _conv_halo(jnp.where(fresh, 0.0, tail[...]), cur_tail)
    buf[halo:halo + CHUNK, :] = raw
    tail[...] = cur_tail
    acc = w[row0 + taps - 1:row0 + taps, :] * raw
    for k in range(taps - 1):
        acc = acc + w[row0 + k:row0 + k + 1, :] * buf[pl.ds(SUBLANES * k, CHUNK), :]
    return acc


def _mod_kernel(c_ref, w_ref, b_ref, o_ref):
    o_ref[...] = jnp.dot(c_ref[...], w_ref[...], precision=lax.Precision.HIGHEST,
                         preferred_element_type=F32) + b_ref[...]


def _modulation(c, w_mod, b_mod):
    bsz = c.shape[0]
    rows = -(-bsz // SUBLANES) * SUBLANES
    c_pad = jnp.zeros((rows, D_MODEL), F32).at[:bsz].set(c)
    out = pl.pallas_call(
        _mod_kernel,
        grid=(DEPTH,),
        in_specs=[
            pl.BlockSpec((rows, D_MODEL), lambda l: (0, 0)),
            pl.BlockSpec((None, D_MODEL, 3 * D_MODEL), lambda l: (l, 0, 0)),
            pl.BlockSpec((None, 1, 3 * D_MODEL), lambda l: (l, 0, 0)),
        ],
        out_specs=pl.BlockSpec((None, rows, 3 * D_MODEL), lambda l: (l, 0, 0)),
        out_shape=jax.ShapeDtypeStruct((DEPTH, rows, 3 * D_MODEL), F32),
        compiler_params=pltpu.CompilerParams(
            dimension_semantics=("arbitrary",), vmem_limit_bytes=VMEM_LIMIT_BYTES),
        name="adaln_modulation",
    )(c_pad, w_mod, b_mod.reshape(DEPTH, 1, 3 * D_MODEL))
    return out


def _mixer_kernel(n_batch, n_chunks, *refs):
    (x_cur, x_nxt,
     mod_ref, wz_ref, wx_ref, wbc_ref, wdt_ref, wsc_ref, gp_ref, hp_ref, convsc_ref, tri_ref, negmask_t_ref,
     perm_ref,
     ynorm_ref, ybin_ref,
     u_ref, un_ref, tailx_ref, tailbc_ref, sctail_ref, tmp_ref, yout_ref, actx_ref, actbc_ref,
     xs_ref, xdtt_ref, xwt_ref, mt_ref, b_ref, ct_ref, tt_ref, ttn_ref, at_ref, atn_ref,
     ac_ref, acn_ref, state_ref) = refs
    L = CHUNK
    DT, TE, EA = range(3)
    batch = pl.program_id(0)
    chunk = pl.program_id(1)
    seq_starts = chunk == 0
    last_chunk = chunk == n_chunks - 1
    nxt_batch = jnp.minimum(batch + last_chunk.astype(jnp.int32), n_batch - 1)

    def norm_and_heads(x_ref, b):
        mod = mod_ref[pl.ds(b, 1), :]
        shift = mod[:, 0:D_MODEL]
        scale = mod[:, D_MODEL:2 * D_MODEL]
        u = (_layer_norm(x_ref[...]) * (1.0 + scale) + shift).astype(BF16)
        un_ref[...] = _dot(perm_ref[...], u).astype(BF16)
        dt = _softplus(_dot(un_ref[...], wdt_ref[...]) + hp_ref[HP_DT_BIAS:HP_DT_BIAS + 1, :])
        a = dt * (-jnp.exp(hp_ref[HP_A_LOG:HP_A_LOG + 1, :]))
        a_hi, a_mid, a_lo = _split3(a)
        tri = tri_ref[...]
        a_cum = (_dot(tri, a_hi.astype(BF16)) + _dot(tri, a_mid.astype(BF16))
                 + _dot(tri, a_lo.astype(BF16)))
        ttn_ref[DT] = dt.T
        ttn_ref[TE] = jnp.exp(a_cum[L - 1:L, :] - a_cum).T
        ttn_ref[EA] = jnp.exp(a_cum).T
        a_cum2 = a_cum * LOG2_E
        acn_ref[...] = a_cum2
        atn_ref[...] = a_cum2.T

    def stage_a(p, slot, u, fresh):
        gp = gp_ref.at[p]
        raw = _dot(u[...], wx_ref[p])
        actx_ref[slot] = _silu(_causal_conv(tmp_ref.at[2 * slot], raw, tailx_ref.at[p], gp, GP_CONVX,
                                            SSM_CONV, fresh) + gp[GP_CONVX_B:GP_CONVX_B + 1, :])
        raw = _dot(u[...], wbc_ref[p])
        actbc_ref[slot] = _silu(_causal_conv(tmp_ref.at[2 * slot + 1], raw, tailbc_ref.at[p], gp, GP_CONVBC,
                                             SSM_CONV, fresh) + gp[GP_CONVBC_B:GP_CONVBC_B + 1, :])

    def head_rows(tab, p):
        return jnp.concatenate(
            [jnp.broadcast_to(tab[pl.ds(p * HEADS_PER_GROUP + hh, 1), :], (HEAD_DIM, L))
             for hh in range(HEADS_PER_GROUP)], axis=0)

    def stage_b(p, src, slot, tt, at, ac):
        act = actx_ref[src]
        xs_ref[slot] = act
        xdt_t = act.T * head_rows(tt.at[DT], p)
        xdtt_ref[slot] = xdt_t.astype(BF16)
        xwt_ref[slot] = (xdt_t * head_rows(tt.at[TE], p)).astype(BF16)
        act = actbc_ref[src]
        bmat = act[:, 0:STATE].astype(BF16)
        cmat = act[:, STATE:2 * STATE].astype(BF16)
        b_ref[slot] = bmat
        ct_ref[slot] = act[:, STATE:2 * STATE].T.astype(BF16)
        bc = lax.dot_general(bmat, cmat, (((1,), (1,)), ((), ())),
                             preferred_element_type=F32)
        for hh in range(HEADS_PER_GROUP):
            h = p * HEADS_PER_GROUP + hh
            a_l = at[pl.ds(h, 1), :]
            a_s = ac[:, h:h + 1]
            seg = a_l - a_s + negmask_t_ref[...]
            mt_ref[slot, hh] = (bc * jnp.exp2(seg)).astype(BF16)

    def stage_c(g, slot):
        y_t = jnp.concatenate(
            [_dot(xdtt_ref[slot, hh * HEAD_DIM:(hh + 1) * HEAD_DIM, :], mt_ref[slot, hh])
             for hh in range(HEADS_PER_GROUP)], axis=0)
        state_t = jnp.where(seq_starts, 0.0, state_ref[g])
        y_t = y_t + _dot(state_t.astype(BF16), ct_ref[slot]) * head_rows(tt_ref.at[EA], g)
        end_decay = jnp.concatenate(
            [jnp.broadcast_to(tt_ref[EA, pl.ds(g * HEADS_PER_GROUP + hh, 1), L - 1:L], (HEAD_DIM, STATE))
             for hh in range(HEADS_PER_GROUP)], axis=0)
        state_ref[g] = state_t * end_decay + _dot(xwt_ref[slot], b_ref[slot])
        gp = gp_ref.at[g]
        y = y_t.T + gp[GP_D_SKIP:GP_D_SKIP + 1, :] * xs_ref[slot]
        y = y * _silu(_dot(u_ref[...], wz_ref[g]))
        y = y * lax.rsqrt(jnp.mean(y * y, axis=-1, keepdims=True) + RMS_EPS)
        _store_time_ordered(yout_ref.at[slot], y * gp[GP_NORM_W:GP_NORM_W + 1, :], ynorm_ref.at[g], slice(None))

    @pl.when((batch == 0) & (chunk == 0))
    def _():
        tailx_ref[...] = jnp.zeros_like(tailx_ref)
        tailbc_ref[...] = jnp.zeros_like(tailbc_ref)
        sctail_ref[...] = jnp.zeros_like(sctail_ref)
        state_ref[...] = jnp.zeros_like(state_ref)
        norm_and_heads(x_cur, batch)
        stage_a(0, 0, un_ref, True)
        stage_a(1, 1, un_ref, True)
        stage_b(0, 0, 0, ttn_ref, atn_ref, acn_ref)

    u_ref[...] = un_ref[...]
    tt_ref[...] = ttn_ref[...]
    at_ref[...] = atn_ref[...]
    ac_ref[...] = acn_ref[...]

    for g in range(0, GROUPS - 2, 2):
        stage_c(g, 0)
        stage_b(g + 1, 1, 1, tt_ref, at_ref, ac_ref)
        stage_a(g + 2, 0, u_ref, seq_starts)
        stage_c(g + 1, 1)
        stage_b(g + 2, 0, 0, tt_ref, at_ref, ac_ref)
        stage_a(g + 3, 1, u_ref, seq_starts)
    stage_c(GROUPS - 2, 0)
    stage_b(GROUPS - 1, 1, 1, tt_ref, at_ref, ac_ref)
    norm_and_heads(x_nxt, nxt_batch)
    stage_a(0, 0, un_ref, last_chunk)
    stage_c(GROUPS - 1, 1)
    stage_b(0, 0, 0, ttn_ref, atn_ref, acn_ref)
    stage_a(1, 1, un_ref, last_chunk)

    for blk in range(SC_WIDTH // COL_BLOCK):
        cols = slice(blk * COL_BLOCK, (blk + 1) * COL_BLOCK)
        u = u_ref[...]
        sc_b = _dot(u, wsc_ref[:, blk * COL_BLOCK:(blk + 1) * COL_BLOCK])
        sc_c = _dot(u, wsc_ref[:, SC_WIDTH + blk * COL_BLOCK:SC_WIDTH + (blk + 1) * COL_BLOCK])
        sc_x = _dot(u, wsc_ref[:, 2 * SC_WIDTH + blk * COL_BLOCK:2 * SC_WIDTH + (blk + 1) * COL_BLOCK])
        sc_g = _dot(u, wsc_ref[:, 3 * SC_WIDTH + blk * COL_BLOCK:3 * SC_WIDTH + (blk + 1) * COL_BLOCK])
        v = _causal_conv(tmp_ref.at[4 + blk % 2], sc_c * sc_x, sctail_ref.at[blk], convsc_ref.at[blk], 0,
                         SC_CONV, seq_starts)
        _store_time_ordered(yout_ref.at[2 + blk % 2], _silu(sc_g) * sc_b * v, ybin_ref, cols)


def _const(shape):
    return pl.BlockSpec(shape, lambda b, i: (0,) * len(shape), pipeline_mode=pl.Buffered(1))


def _layer_const(layer, shape):
    return pl.BlockSpec((None,) + tuple(shape[1:]), lambda b, i: (layer,) + (0,) * (len(shape) - 1),
                        pipeline_mode=pl.Buffered(1))


def _mixer(layer, x, layer_consts, consts):
    bsz, seq, _ = x.shape
    L = CHUNK
    assert seq % L == 0
    n_chunks = seq // L

    def next_block(b, i):
        wraps = (i == n_chunks - 1).astype(jnp.int32)
        return (jnp.minimum(b + wraps, bsz - 1), (i + 1) * (1 - wraps), 0)

    return pl.pallas_call(
        functools.partial(_mixer_kernel, bsz, n_chunks),
        grid=(bsz, n_chunks),
        in_specs=[
            pl.BlockSpec((None, L, D_MODEL), lambda b, i: (b, i, 0)),
            pl.BlockSpec((None, L, D_MODEL), next_block),
        ] + [_layer_const(layer, a.shape) for a in layer_consts] + [_const(a.shape) for a in consts],
        out_specs=[
            pl.BlockSpec((None, GROUPS, L, GROUP_WIDTH), lambda b, i: (b, 0, i, 0)),
            pl.BlockSpec((None, L, SC_WIDTH), lambda b, i: (b, i, 0)),
        ],
        out_shape=[
            jax.ShapeDtypeStruct((bsz, GROUPS, seq, GROUP_WIDTH), BF16),
            jax.ShapeDtypeStruct((bsz, seq, SC_WIDTH), BF16),
        ],
        scratch_shapes=[
            pltpu.VMEM((L, D_MODEL), BF16),
            pltpu.VMEM((L, D_MODEL), BF16),
            pltpu.VMEM((GROUPS, SSM_HALO, GROUP_WIDTH), F32),
            pltpu.VMEM((GROUPS, SSM_HALO, 2 * STATE), F32),
            pltpu.VMEM((SC_WIDTH // COL_BLOCK, SC_HALO, COL_BLOCK), F32),
            pltpu.VMEM((6, SSM_HALO + L, COL_BLOCK), F32),
            pltpu.VMEM((4, COL_BLOCK // LANES, L, LANES), F32),
            pltpu.VMEM((2, L, GROUP_WIDTH), F32),
            pltpu.VMEM((2, L, 2 * STATE), F32),
            pltpu.VMEM((2, L, GROUP_WIDTH), F32),
            pltpu.VMEM((2, GROUP_WIDTH, L), BF16),
            pltpu.VMEM((2, GROUP_WIDTH, L), BF16),
            pltpu.VMEM((2, HEADS_PER_GROUP, L, L), BF16),
            pltpu.VMEM((2, L, STATE), BF16),
            pltpu.VMEM((2, STATE, L), BF16),
            pltpu.VMEM((3, LANES, L), F32),
            pltpu.VMEM((3, LANES, L), F32),
            pltpu.VMEM((LANES, L), F32),
            pltpu.VMEM((LANES, L), F32),
            pltpu.VMEM((L, LANES), F32),
            pltpu.VMEM((L, LANES), F32),
            pltpu.VMEM((GROUPS, GROUP_WIDTH, STATE), F32),
        ],
        compiler_params=pltpu.CompilerParams(
            dimension_semantics=("arbitrary", "arbitrary"), vmem_limit_bytes=VMEM_LIMIT_BYTES),
        name="ssd_shortconv_mixer",
    )(x, x, *layer_consts, *consts)


def _merge_kernel(x_ref, ynorm_ref, ybin_ref, mod_ref, wgate_ref, bgate_ref, wa_ref, wb_ref,
                  wout_ref, ln_ref, o_ref):
    x = x_ref[...]
    mod = mod_ref[pl.ds(pl.program_id(0), 1), :]
    shift = mod[:, 0:D_MODEL]
    scale = mod[:, D_MODEL:2 * D_MODEL]
    gate = mod[:, 2 * D_MODEL:3 * D_MODEL]
    u = (_layer_norm(x) * (1.0 + scale) + shift).astype(BF16)
    gates = jax.nn.sigmoid(_dot(u, wgate_ref[...]) + bgate_ref[...])
    y_a = _dot(ynorm_ref[0], wa_ref[0:GROUP_WIDTH, :])
    for g in range(1, GROUPS):
        y_a = y_a + _dot(ynorm_ref[g], wa_ref[g * GROUP_WIDTH:(g + 1) * GROUP_WIDTH, :])
    y_b = _dot(ybin_ref[...], wb_ref[...])
    merged = gates[:, 0:D_MODEL] * y_a + gates[:, D_MODEL:2 * D_MODEL] * y_b
    mixed = _dot(merged.astype(BF16), wout_ref[...])
    o_ref[...] = _layer_norm(DN_ALPHA * x + gate * mixed) * ln_ref[0:1, :] + ln_ref[1:2, :]


def _merge(layer, x, ynorm, ybin, layer_consts):
    bsz, seq, _ = x.shape
    tm = min(MERGE_TILE, seq)
    assert seq % tm == 0
    return pl.pallas_call(
        _merge_kernel,
        grid=(bsz, seq // tm),
        in_specs=[
            pl.BlockSpec((None, tm, D_MODEL), lambda b, i: (b, i, 0)),
            pl.BlockSpec((None, GROUPS, tm, GROUP_WIDTH), lambda b, i: (b, 0, i, 0)),
            pl.BlockSpec((None, tm, SC_WIDTH), lambda b, i: (b, i, 0)),
        ] + [_layer_const(layer, a.shape) for a in layer_consts],
        out_specs=pl.BlockSpec((None, tm, D_MODEL), lambda b, i: (b, i, 0)),
        out_shape=jax.ShapeDtypeStruct((bsz, seq, D_MODEL), F32),
        compiler_params=pltpu.CompilerParams(
            dimension_semantics=("arbitrary", "arbitrary"), vmem_limit_bytes=VMEM_LIMIT_BYTES),
        name="gated_merge",
    )(x, ynorm, ybin, *layer_consts)


def _head_lanes(v):
    pad = jnp.zeros((v.shape[0], LANES - HEAD_REPLICAS * HEADS), F32)
    return jnp.concatenate([v] * HEAD_REPLICAS + [pad], axis=1)


def _selection_constants():
    L = CHUNK
    idx = jnp.arange(L)
    time = ROW_TILES * (idx % SUBLANES) + idx // SUBLANES
    causal = time[:, None] >= time[None, :]
    tri = causal.astype(BF16)
    negmask_t = jnp.where(causal.T, 0.0, -jnp.inf).astype(F32)
    perm = (time[:, None] == idx[None, :]).astype(BF16)
    return tri, negmask_t, perm


def _by_group(a, width):
    return a.reshape(a.shape[0], a.shape[1], GROUPS, width).transpose(0, 2, 1, 3)


def _bc_by_group(a):
    b = a[..., D_INNER:D_INNER + GROUPS * STATE].reshape(a.shape[0], a.shape[1], GROUPS, STATE)
    c = a[..., D_INNER + GROUPS * STATE:].reshape(a.shape[0], a.shape[1], GROUPS, STATE)
    return jnp.concatenate([b, c], axis=3).transpose(0, 2, 1, 3)


def kernel(x, c, w_mod, b_mod, w_in, conv_ssm_w, conv_ssm_b, dt_bias, a_log, d_skip, ssm_norm_w,
           conv_sc_w, w_branch_a, w_branch_b, w_gate, b_gate, w_out, ln_g, ln_b):
    mod = _modulation(c, w_mod, b_mod)
    wz = _by_group(w_in[:, :, :SPLIT_Z].astype(BF16), GROUP_WIDTH)
    wx = _by_group(w_in[:, :, SPLIT_Z:SPLIT_Z + D_INNER].astype(BF16), GROUP_WIDTH)
    wbc = _bc_by_group(w_in[:, :, SPLIT_Z:SPLIT_XBC].astype(BF16))
    wdt_h = w_in[:, :, SPLIT_XBC:SPLIT_DT].astype(BF16)
    wdt = jnp.concatenate(
        [wdt_h] * HEAD_REPLICAS + [jnp.zeros((DEPTH, D_MODEL, LANES - HEAD_REPLICAS * HEADS), BF16)], axis=2)
    wsc = w_in[:, :, SPLIT_DT:].astype(BF16)
    conv_b = conv_ssm_b.reshape(DEPTH, 1, XBC)
    group_params = jnp.concatenate([
        _by_group(conv_ssm_w[:, :, :D_INNER], GROUP_WIDTH), _by_group(conv_b[:, :, :D_INNER], GROUP_WIDTH),
        _bc_by_group(conv_ssm_w), _bc_by_group(conv_b),
        jnp.repeat(d_skip, HEAD_DIM, axis=1).reshape(DEPTH, GROUPS, 1, GROUP_WIDTH),
        ssm_norm_w.reshape(DEPTH, GROUPS, 1, GROUP_WIDTH)], axis=2)
    head_params = jnp.stack([_head_lanes(dt_bias), _head_lanes(a_log)], axis=1)
    conv_sc = conv_sc_w.reshape(DEPTH, SC_CONV, SC_WIDTH // COL_BLOCK, COL_BLOCK).transpose(0, 2, 1, 3)
    mixer_consts = (mod, wz, wx, wbc, wdt, wsc, group_params, head_params, conv_sc)
    merge_consts = (mod, w_gate.astype(BF16), b_gate.reshape(DEPTH, 1, 2 * D_MODEL), w_branch_a.astype(BF16),
                    w_branch_b.astype(BF16), w_out.astype(BF16), jnp.stack([ln_g, ln_b], axis=1))
    shared = _selection_constants()
    for l in range(DEPTH):
        ynorm, ybin = _mixer(l, x, mixer_consts, shared)
        x = _merge(l, x, ynorm, ybin, merge_consts)
    return x
```

```python
import functools

import jax
import jax.numpy as jnp
from jax import lax
from jax.experimental import pallas as pl
from jax.experimental.pallas import tpu as pltpu

D_MODEL = 1024
DEPTH = 2
D_INNER = 2048
HEAD_DIM = 64
HEADS = 32
GROUPS = 8
STATE = 128
SSM_CONV = 4
CHUNK = 256
XBC = D_INNER + 2 * GROUPS * STATE
SC_WIDTH = 1024
SC_CONV = 3
HEADS_PER_GROUP = HEADS // GROUPS
GROUP_WIDTH = HEADS_PER_GROUP * HEAD_DIM
SPLIT_Z = D_INNER
SPLIT_XBC = SPLIT_Z + XBC
SPLIT_DT = SPLIT_XBC + HEADS
DN_ALPHA = (2 * DEPTH) ** 0.25
LN_EPS = 1e-5
RMS_EPS = 1e-5
LOG2_E = 1.4426950408889634

LANES = 128
SUBLANES = 8
ROW_TILES = CHUNK // SUBLANES
SSM_HALO = SUBLANES * (SSM_CONV - 1)
SC_HALO = SUBLANES * (SC_CONV - 1)
COL_BLOCK = 256
HEAD_REPLICAS = 3
GP_CONVX, GP_CONVX_B, GP_CONVBC, GP_CONVBC_B, GP_D_SKIP, GP_NORM_W = 0, 4, 5, 9, 10, 11
HP_DT_BIAS, HP_A_LOG = 0, 1
MERGE_TILE = 1024
MERGE_SUBTILE = 512
VMEM_LIMIT_BYTES = 56 * 1024 * 1024

F32 = jnp.float32
BF16 = jnp.bfloat16


def _dot(a, b):
    return jnp.dot(a, b, preferred_element_type=F32)


def _layer_norm(x):
    mu = jnp.mean(x, axis=-1, keepdims=True)
    xc = x - mu
    var = jnp.mean(xc * xc, axis=-1, keepdims=True)
    return xc * lax.rsqrt(var + LN_EPS)


def _silu(x):
    h = 0.5 * x
    return h + h * jnp.tanh(h)


def _softplus(x):
    return jnp.maximum(x, 0.0) + jnp.log1p(jnp.exp(-jnp.abs(x)))


def _split3(x):
    hi = x.astype(BF16).astype(F32)
    r = x - hi
    mid = r.astype(BF16).astype(F32)
    return hi, mid, r - mid


def _lane_parts(x):
    hi, mid, lo = _split3(x)
    lane = lax.broadcasted_iota(jnp.int32, x.shape, 1)
    return jnp.where(lane < HEADS, hi, jnp.where(lane < 2 * HEADS, mid, lo)).astype(BF16)


def _store_time_ordered(stage_ref, value, out_ref, cols):
    for j in range(value.shape[1] // LANES):
        stage_ref[j] = value[:, j * LANES:(j + 1) * LANES]
    out_ref[:, cols] = jnp.concatenate(
        [jnp.concatenate([stage_ref.at[j][pl.ds(s, ROW_TILES, stride=SUBLANES), :]
                          for s in range(SUBLANES)], axis=0)
         for j in range(value.shape[1] // LANES)], axis=1).astype(out_ref.dtype)


def _conv_halo(prev_tail, cur_tail):
    rows = cur_tail.shape[0]
    sub = lax.broadcasted_iota(jnp.int32, cur_tail.shape, 0) % SUBLANES
    return jnp.where(sub == 0,
                     pltpu.roll(prev_tail, rows - (SUBLANES - 1), axis=0),
                     pltpu.roll(cur_tail, 1, axis=0))


def _causal_conv(buf, raw, tail, w, row0, taps, fresh):
    halo = SUBLANES * (taps - 1)
    cur_tail = raw[CHUNK - halo:CHUNK, :]
    buf[0:halo, :] = _conv_halo(jnp.where(fresh, 0.0, tail[...]), cur_tail)
    buf[halo:halo + CHUNK, :] = raw
    tail[...] = cur_tail
    acc = w[row0 + taps - 1:row0 + taps, :] * raw
    for k in range(taps - 1):
        acc = acc + w[row0 + k:row0 + k + 1, :] * buf[pl.ds(SUBLANES * k, CHUNK), :]
    return acc


def _mod_kernel(c_ref, w_ref, b_ref, o_ref):
    o_ref[...] = jnp.dot(c_ref[...], w_ref[...], precision=lax.Precision.HIGHEST,
                         preferred_element_type=F32) + b_ref[...]


def _modulation(c, w_mod, b_mod):
    bsz = c.shape[0]
    rows = -(-bsz // SUBLANES) * SUBLANES
    c_pad = jnp.zeros((rows, D_MODEL), F32).at[:bsz].set(c)
    out = pl.pallas_call(
        _mod_kernel,
        grid=(DEPTH,),
        in_specs=[
            pl.BlockSpec((rows, D_MODEL), lambda l: (0, 0)),
            pl.BlockSpec((None, D_MODEL, 3 * D_MODEL), lambda l: (l, 0, 0)),
            pl.BlockSpec((None, 1, 3 * D_MODEL), lambda l: (l, 0, 0)),
        ],
        out_specs=pl.BlockSpec((None, rows, 3 * D_MODEL), lambda l: (l, 0, 0)),
        out_shape=jax.ShapeDtypeStruct((DEPTH, rows, 3 * D_MODEL), F32),
        compiler_params=pltpu.CompilerParams(
            dimension_semantics=("arbitrary",), vmem_limit_bytes=VMEM_LIMIT_BYTES),
        name="adaln_modulation",
    )(c_pad, w_mod, b_mod.reshape(DEPTH, 1, 3 * D_MODEL))
    return out


def _mixer_kernel(n_batch, n_chunks, *refs):
    (x_cur, x_nxt,
     mod_ref, wz_ref, wx_ref, wbc_ref, wdt_ref, wsc_ref, gp_ref, hp_ref, convsc_ref, tri_ref, negmask_t_ref,
     eexp_ref, perm_ref,
     ynorm_ref, ybin_ref,
     u_ref, un_ref, tailx_ref, tailbc_ref, sctail_ref, tmp_ref, yout_ref, actx_ref, actbc_ref,
     xs_ref, xdtt_ref, xw_ref, mt_ref, bt_ref, c_ref, hb_ref, hbn_ref, at_ref, atn_ref,
     ac_ref, acn_ref, state_ref) = refs
    L = CHUNK
    DT3, TE3, E3 = range(3)
    batch = pl.program_id(0)
    chunk = pl.program_id(1)
    seq_starts = chunk == 0
    last_chunk = chunk == n_chunks - 1
    nxt_batch = jnp.minimum(batch + last_chunk.astype(jnp.int32), n_batch - 1)

    def norm_and_heads(x_ref, b):
        mod = mod_ref[pl.ds(b, 1), :]
        shift = mod[:, 0:D_MODEL]
        scale = mod[:, D_MODEL:2 * D_MODEL]
        u = (_layer_norm(x_ref[...]) * (1.0 + scale) + shift).astype(BF16)
        un_ref[...] = _dot(perm_ref[...], u).astype(BF16)
        dt = _softplus(_dot(un_ref[...], wdt_ref[...]) + hp_ref[HP_DT_BIAS:HP_DT_BIAS + 1, :])
        a = dt * (-jnp.exp(hp_ref[HP_A_LOG:HP_A_LOG + 1, :]))
        a_hi, a_mid, a_lo = _split3(a)
        tri = tri_ref[...]
        a_cum = (_dot(tri, a_hi.astype(BF16)) + _dot(tri, a_mid.astype(BF16))
                 + _dot(tri, a_lo.astype(BF16)))
        hbn_ref[DT3] = _lane_parts(dt)
        hbn_ref[TE3] = _lane_parts(jnp.exp(a_cum[L - 1:L, :] - a_cum))
        hbn_ref[E3] = _lane_parts(jnp.exp(a_cum))
        a_cum2 = a_cum * LOG2_E
        acn_ref[...] = a_cum2
        atn_ref[...] = a_cum2.T

    def stage_a(p, slot, u, fresh):
        gp = gp_ref.at[p]
        raw = _dot(u[...], wx_ref[p])
        actx_ref[slot] = _silu(_causal_conv(tmp_ref.at[2 * slot], raw, tailx_ref.at[p], gp, GP_CONVX,
                                            SSM_CONV, fresh) + gp[GP_CONVX_B:GP_CONVX_B + 1, :])
        raw = _dot(u[...], wbc_ref[p])
        actbc_ref[slot] = _silu(_causal_conv(tmp_ref.at[2 * slot + 1], raw, tailbc_ref.at[p], gp, GP_CONVBC,
                                             SSM_CONV, fresh) + gp[GP_CONVBC_B:GP_CONVBC_B + 1, :])

    def stage_b(p, src, slot, hb, at, ac):
        act = actx_ref[src]
        xs_ref[slot] = act
        xdt = act * _dot(hb[DT3], eexp_ref[p])
        xdtt_ref[slot] = xdt.T.astype(BF16)
        xw_ref[slot] = (xdt * _dot(hb[TE3], eexp_ref[p])).astype(BF16)
        act = actbc_ref[src]
        bmat = act[:, 0:STATE].astype(BF16)
        cmat = act[:, STATE:2 * STATE].astype(BF16)
        bt_ref[slot] = act[:, 0:STATE].T.astype(BF16)
        c_ref[slot] = cmat
        bc = lax.dot_general(bmat, cmat, (((1,), (1,)), ((), ())),
                             preferred_element_type=F32)
        for hh in range(HEADS_PER_GROUP):
            h = p * HEADS_PER_GROUP + hh
            a_l = at[pl.ds(h, 1), :]
            a_s = ac[:, h:h + 1]
            seg = a_l - a_s + negmask_t_ref[...]
            mt_ref[slot, hh] = (bc * jnp.exp2(seg)).astype(BF16)

    def stage_c(g, slot):
        y = jnp.concatenate(
            [_dot(xdtt_ref[slot, hh * HEAD_DIM:(hh + 1) * HEAD_DIM, :], mt_ref[slot, hh])
             for hh in range(HEADS_PER_GROUP)], axis=0).T
        state = jnp.where(seq_starts, 0.0, state_ref[g])
        escale = _dot(hb_ref[E3], eexp_ref[g])
        y = y + _dot(c_ref[slot], state.astype(BF16)) * escale
        state_ref[g] = state * escale[L - 1:L, :] + _dot(bt_ref[slot], xw_ref[slot])
        gp = gp_ref.at[g]
        y = y + gp[GP_D_SKIP:GP_D_SKIP + 1, :] * xs_ref[slot]
        y = y * _silu(_dot(u_ref[...], wz_ref[g]))
        y = y * lax.rsqrt(jnp.mean(y * y, axis=-1, keepdims=True) + RMS_EPS)
        _store_time_ordered(yout_ref.at[slot], y * gp[GP_NORM_W:GP_NORM_W + 1, :], ynorm_ref.at[g], slice(None))

    @pl.when((batch == 0) & (chunk == 0))
    def _():
        tailx_ref[...] = jnp.zeros_like(tailx_ref)
        tailbc_ref[...] = jnp.zeros_like(tailbc_ref)
        sctail_ref[...] = jnp.zeros_like(sctail_ref)
        state_ref[...] = jnp.zeros_like(state_ref)
        norm_and_heads(x_cur, batch)
        stage_a(0, 0, un_ref, True)
        stage_a(1, 1, un_ref, True)
        stage_b(0, 0, 0, hbn_ref, atn_ref, acn_ref)

    u_ref[...] = un_ref[...]
    hb_ref[...] = hbn_ref[...]
    at_ref[...] = atn_ref[...]
    ac_ref[...] = acn_ref[...]

    for g in range(0, GROUPS - 2, 2):
        stage_c(g, 0)
        stage_b(g + 1, 1, 1, hb_ref, at_ref, ac_ref)
        stage_a(g + 2, 0, u_ref, seq_starts)
        stage_c(g + 1, 1)
        stage_b(g + 2, 0, 0, hb_ref, at_ref, ac_ref)
        stage_a(g + 3, 1, u_ref, seq_starts)
    stage_c(GROUPS - 2, 0)
    stage_b(GROUPS - 1, 1, 1, hb_ref, at_ref, ac_ref)
    norm_and_heads(x_nxt, nxt_batch)
    stage_a(0, 0, un_ref, last_chunk)
    stage_c(GROUPS - 1, 1)
    stage_b(0, 0, 0, hbn_ref, atn_ref, acn_ref)
    stage_a(1, 1, un_ref, last_chunk)

    for blk in range(SC_WIDTH // COL_BLOCK):
        cols = slice(blk * COL_BLOCK, (blk + 1) * COL_BLOCK)
        u = u_ref[...]
        sc_b = _dot(u, wsc_ref[:, blk * COL_BLOCK:(blk + 1) * COL_BLOCK])
        sc_c = _dot(u, wsc_ref[:, SC_WIDTH + blk * COL_BLOCK:SC_WIDTH + (blk + 1) * COL_BLOCK])
        sc_x = _dot(u, wsc_ref[:, 2 * SC_WIDTH + blk * COL_BLOCK:2 * SC_WIDTH + (blk + 1) * COL_BLOCK])
        sc_g = _dot(u, wsc_ref[:, 3 * SC_WIDTH + blk * COL_BLOCK:3 * SC_WIDTH + (blk + 1) * COL_BLOCK])
        v = _causal_conv(tmp_ref.at[4 + blk % 2], sc_c * sc_x, sctail_ref.at[blk], convsc_ref.at[blk], 0,
                         SC_CONV, seq_starts)
        _store_time_ordered(yout_ref.at[2 + blk % 2], _silu(sc_g) * sc_b * v, ybin_ref, cols)


def _const(shape):
    return pl.BlockSpec(shape, lambda b, i: (0,) * len(shape), pipeline_mode=pl.Buffered(1))


def _layer_const(layer, shape):
    return pl.BlockSpec((None,) + tuple(shape[1:]), lambda b, i: (layer,) + (0,) * (len(shape) - 1),
                        pipeline_mode=pl.Buffered(1))


def _mixer(layer, x, layer_consts, consts):
    bsz, seq, _ = x.shape
    L = CHUNK
    assert seq % L == 0
    n_chunks = seq // L

    def next_block(b, i):
        wraps = (i == n_chunks - 1).astype(jnp.int32)
        return (jnp.minimum(b + wraps, bsz - 1), (i + 1) * (1 - wraps), 0)

    return pl.pallas_call(
        functools.partial(_mixer_kernel, bsz, n_chunks),
        grid=(bsz, n_chunks),
        in_specs=[
            pl.BlockSpec((None, L, D_MODEL), lambda b, i: (b, i, 0)),
            pl.BlockSpec((None, L, D_MODEL), next_block),
        ] + [_layer_const(layer, a.shape) for a in layer_consts] + [_const(a.shape) for a in consts],
        out_specs=[
            pl.BlockSpec((None, GROUPS, L, GROUP_WIDTH), lambda b, i: (b, 0, i, 0)),
            pl.BlockSpec((None, L, SC_WIDTH), lambda b, i: (b, i, 0)),
        ],
        out_shape=[
            jax.ShapeDtypeStruct((bsz, GROUPS, seq, GROUP_WIDTH), BF16),
            jax.ShapeDtypeStruct((bsz, seq, SC_WIDTH), BF16),
        ],
        scratch_shapes=[
            pltpu.VMEM((L, D_MODEL), BF16),
            pltpu.VMEM((L, D_MODEL), BF16),
            pltpu.VMEM((GROUPS, SSM_HALO, GROUP_WIDTH), F32),
            pltpu.VMEM((GROUPS, SSM_HALO, 2 * STATE), F32),
            pltpu.VMEM((SC_WIDTH // COL_BLOCK, SC_HALO, COL_BLOCK), F32),
            pltpu.VMEM((6, SSM_HALO + L, COL_BLOCK), F32),
            pltpu.VMEM((4, COL_BLOCK // LANES, L, LANES), F32),
            pltpu.VMEM((2, L, GROUP_WIDTH), F32),
            pltpu.VMEM((2, L, 2 * STATE), F32),
            pltpu.VMEM((2, L, GROUP_WIDTH), F32),
            pltpu.VMEM((2, GROUP_WIDTH, L), BF16),
            pltpu.VMEM((2, L, GROUP_WIDTH), BF16),
            pltpu.VMEM((2, HEADS_PER_GROUP, L, L), BF16),
            pltpu.VMEM((2, STATE, L), BF16),
            pltpu.VMEM((2, L, STATE), BF16),
            pltpu.VMEM((3, L, LANES), BF16),
            pltpu.VMEM((3, L, LANES), BF16),
            pltpu.VMEM((LANES, L), F32),
            pltpu.VMEM((LANES, L), F32),
            pltpu.VMEM((L, LANES), F32),
            pltpu.VMEM((L, LANES), F32),
            pltpu.VMEM((GROUPS, STATE, GROUP_WIDTH), F32),
        ],
        compiler_params=pltpu.CompilerParams(
            dimension_semantics=("arbitrary", "arbitrary"), vmem_limit_bytes=VMEM_LIMIT_BYTES),
        name="ssd_shortconv_mixer",
    )(x, x, *layer_consts, *consts)


def _merge_kernel(x_ref, ynorm_ref, ybin_ref, mod_ref, wgate_ref, bgate_ref, wa_ref, wb_ref,
                  wout_ref, ln_ref, o_ref):
    mod = mod_ref[pl.ds(pl.program_id(0), 1), :]
    shift = mod[:, 0:D_MODEL]
    scale = mod[:, D_MODEL:2 * D_MODEL]
    gate = mod[:, 2 * D_MODEL:3 * D_MODEL]
    for rows in (slice(r, r + MERGE_SUBTILE) for r in range(0, x_ref.shape[0], MERGE_SUBTILE)):
        x = x_ref[rows, :]
        u = (_layer_norm(x) * (1.0 + scale) + shift).astype(BF16)
        gates = jax.nn.sigmoid(_dot(u, wgate_ref[...]) + bgate_ref[...])
        y_a = _dot(ynorm_ref[0, rows, :], wa_ref[0:GROUP_WIDTH, :])
        for g in range(1, GROUPS):
            y_a = y_a + _dot(ynorm_ref[g, rows, :], wa_ref[g * GROUP_WIDTH:(g + 1) * GROUP_WIDTH, :])
        y_b = _dot(ybin_ref[rows, :], wb_ref[...])
        merged = gates[:, 0:D_MODEL] * y_a + gates[:, D_MODEL:2 * D_MODEL] * y_b
        mixed = _dot(merged.astype(BF16), wout_ref[...])
        o_ref[rows, :] = _layer_norm(DN_ALPHA * x + gate * mixed) * ln_ref[0:1, :] + ln_ref[1:2, :]


def _merge(layer, x, ynorm, ybin, layer_consts):
    bsz, seq, _ = x.shape
    tm = min(MERGE_TILE, seq)
    assert seq % tm == 0 and tm % MERGE_SUBTILE == 0
    return pl.pallas_call(
        _merge_kernel,
        grid=(bsz, seq // tm),
        in_specs=[
            pl.BlockSpec((None, tm, D_MODEL), lambda b, i: (b, i, 0)),
            pl.BlockSpec((None, GROUPS, tm, GROUP_WIDTH), lambda b, i: (b, 0, i, 0)),
            pl.BlockSpec((None, tm, SC_WIDTH), lambda b, i: (b, i, 0)),
        ] + [_layer_const(layer, a.shape) for a in layer_consts],
        out_specs=pl.BlockSpec((None, tm, D_MODEL), lambda b, i: (b, i, 0)),
        out_shape=jax.ShapeDtypeStruct((bsz, seq, D_MODEL), F32),
        compiler_params=pltpu.CompilerParams(
            dimension_semantics=("arbitrary", "arbitrary"), vmem_limit_bytes=VMEM_LIMIT_BYTES),
        name="gated_merge",
    )(x, ynorm, ybin, *layer_consts)


def _head_lanes(v):
    pad = jnp.zeros((v.shape[0], LANES - HEAD_REPLICAS * HEADS), F32)
    return jnp.concatenate([v] * HEAD_REPLICAS + [pad], axis=1)


def _selection_constants():
    L = CHUNK
    idx = jnp.arange(L)
    time = ROW_TILES * (idx % SUBLANES) + idx // SUBLANES
    causal = time[:, None] >= time[None, :]
    tri = causal.astype(BF16)
    negmask_t = jnp.where(causal.T, 0.0, -jnp.inf).astype(F32)
    j = jnp.arange(LANES)
    head_of_lane = jnp.where(j < HEAD_REPLICAS * HEADS, j % HEADS, -1)
    ch_head = jnp.arange(D_INNER) // HEAD_DIM
    eexp = (head_of_lane[:, None] == ch_head[None, :]).astype(BF16)
    eexp = eexp.reshape(LANES, GROUPS, GROUP_WIDTH).transpose(1, 0, 2)
    perm = (time[:, None] == idx[None, :]).astype(BF16)
    return tri, negmask_t, eexp, perm


def _by_group(a, width):
    return a.reshape(a.shape[0], a.shape[1], GROUPS, width).transpose(0, 2, 1, 3)


def _bc_by_group(a):
    b = a[..., D_INNER:D_INNER + GROUPS * STATE].reshape(a.shape[0], a.shape[1], GROUPS, STATE)
    c = a[..., D_INNER + GROUPS * STATE:].reshape(a.shape[0], a.shape[1], GROUPS, STATE)
    return jnp.concatenate([b, c], axis=3).transpose(0, 2, 1, 3)


def kernel(x, c, w_mod, b_mod, w_in, conv_ssm_w, conv_ssm_b, dt_bias, a_log, d_skip, ssm_norm_w,
           conv_sc_w, w_branch_a, w_branch_b, w_gate, b_gate, w_out, ln_g, ln_b):
    mod = _modulation(c, w_mod, b_mod)
    wz = _by_group(w_in[:, :, :SPLIT_Z].astype(BF16), GROUP_WIDTH)
    wx = _by_group(w_in[:, :, SPLIT_Z:SPLIT_Z + D_INNER].astype(BF16), GROUP_WIDTH)
    wbc = _bc_by_group(w_in[:, :, SPLIT_Z:SPLIT_XBC].astype(BF16))
    wdt_h = w_in[:, :, SPLIT_XBC:SPLIT_DT].astype(BF16)
    wdt = jnp.concatenate(
        [wdt_h] * HEAD_REPLICAS + [jnp.zeros((DEPTH, D_MODEL, LANES - HEAD_REPLICAS * HEADS), BF16)], axis=2)
    wsc = w_in[:, :, SPLIT_DT:].astype(BF16)
    conv_b = conv_ssm_b.reshape(DEPTH, 1, XBC)
    group_params = jnp.concatenate([
        _by_group(conv_ssm_w[:, :, :D_INNER], GROUP_WIDTH), _by_group(conv_b[:, :, :D_INNER], GROUP_WIDTH),
        _bc_by_group(conv_ssm_w), _bc_by_group(conv_b),
        jnp.repeat(d_skip, HEAD_DIM, axis=1).reshape(DEPTH, GROUPS, 1, GROUP_WIDTH),
        ssm_norm_w.reshape(DEPTH, GROUPS, 1, GROUP_WIDTH)], axis=2)
    head_params = jnp.stack([_head_lanes(dt_bias), _head_lanes(a_log)], axis=1)
    conv_sc = conv_sc_w.reshape(DEPTH, SC_CONV, SC_WIDTH // COL_BLOCK, COL_BLOCK).transpose(0, 2, 1, 3)
    mixer_consts = (mod, wz, wx, wbc, wdt, wsc, group_params, head_params, conv_sc)
    merge_consts = (mod, w_gate.astype(BF16), b_gate.reshape(DEPTH, 1, 2 * D_MODEL), w_branch_a.astype(BF16),
                    w_branch_b.astype(BF16), w_out.astype(BF16), jnp.stack([ln_g, ln_b], axis=1))
    shared = _selection_constants()
    for l in range(DEPTH):
        ynorm, ybin = _mixer(l, x, mixer_consts, shared)
        x = _merge(l, x, ynorm, ybin, merge_consts)
    return x
```
